```python
import jax, jax.numpy as jnp
from jax import lax
import numpy as np

D_MODEL = 1024
BATCH = 4
SEQ = 8192
DEPTH = 1

CHUNK = 64
N_MEM = 256
MEM_HEADS = 4
MEM_HEAD_DIM = D_MODEL // MEM_HEADS
MIX_WIDTH = D_MODEL
HG_WIDTH = MIX_WIDTH // 2
HG_HEADS = 4
HG_HEAD_DIM = HG_WIDTH // HG_HEADS
SB_WIDTH = MIX_WIDTH - HG_WIDTH
SB_HEADS = 8
SB_HEAD_DIM = SB_WIDTH // SB_HEADS
SB_BLOCK = 128
D_FF = 2816
IN_COLS = 4 * HG_WIDTH + 3 * SB_WIDTH
EPS = 1e-6

kernel_name = "hybrid_hgrn2_stickbreaking_macaron_layer"


def rmsnorm(x, g):
    xf = x.astype(jnp.float32)
    y = xf * lax.rsqrt(jnp.mean(xf * xf, axis=-1, keepdims=True) + EPS)
    return (y * g.astype(jnp.float32)).astype(x.dtype)


def head_rmsnorm(o, g):
    B, S, H, d = o.shape
    of = o.astype(jnp.float32)
    y = of * lax.rsqrt(jnp.mean(of * of, axis=-1, keepdims=True) + EPS)
    return y * g.astype(jnp.float32).reshape(H, d)


def swiglu(h, w_gu, w_down):
    gu = h @ w_gu
    gate, up = jnp.split(gu, 2, axis=-1)
    return (jax.nn.silu(gate) * up) @ w_down


def hgrn2_mixer(q, f_logit, i, g, lb, gnorm):
    B, S, _ = q.shape
    n_chunks = S // CHUNK
    lbf = lb.astype(jnp.float32)
    f = lbf + (1.0 - lbf) * jax.nn.sigmoid(f_logit.astype(jnp.float32))
    k = 1.0 - f
    logf = jnp.log(f)

    def to_chunks(t):
        return t.astype(jnp.float32).reshape(B, n_chunks, CHUNK, HG_HEADS, HG_HEAD_DIM).transpose(1, 0, 3, 2, 4)

    qc, kc, vc = to_chunks(q), to_chunks(k), to_chunks(i)
    bc = jnp.cumsum(to_chunks(logf), axis=3)
    causal = jnp.tril(jnp.ones((CHUNK, CHUNK), dtype=bool))[None, None, :, :, None]

    def step(state, inp):
        qt, kt, vt, bt = inp
        diff = bt[:, :, :, None, :] - bt[:, :, None, :, :]
        decay = jnp.exp(jnp.where(causal, diff, -jnp.inf))
        a = jnp.einsum('bhtd,bhsd,bhtsd->bhts', qt, kt, decay)
        o_intra = jnp.einsum('bhts,bhsv->bhtv', a, vt)
        o_inter = jnp.einsum('bhtd,bhdv->bhtv', qt * jnp.exp(bt), state)
        b_last = bt[:, :, -1:, :]
        k_dec = kt * jnp.exp(b_last - bt)
        new_state = jnp.exp(b_last[:, :, 0, :])[..., None] * state + jnp.einsum('bhsd,bhsv->bhdv', k_dec, vt)
        return new_state, o_intra + o_inter

    s0 = jnp.zeros((B, HG_HEADS, HG_HEAD_DIM, HG_HEAD_DIM), jnp.float32)
    _, o = lax.scan(step, s0, (qc, kc, vc, bc))
    o = o.transpose(1, 0, 3, 2, 4).reshape(B, S, HG_HEADS, HG_HEAD_DIM)
    o = head_rmsnorm(o, gnorm).reshape(B, S, HG_WIDTH)
    return o * jax.nn.silu(g.astype(jnp.float32))


def stick_breaking_mixer(q, k, v, gnorm):
    B, S, _ = q.shape
    n_blk = S // SB_BLOCK
    scale = SB_HEAD_DIM ** -0.5

    def heads(t):
        return t.astype(jnp.float32).reshape(B, S, SB_HEADS, SB_HEAD_DIM).transpose(0, 2, 1, 3)

    qh, kh, vh = heads(q), heads(k), heads(v)
    qb = qh.reshape(B, SB_HEADS, n_blk, SB_BLOCK, SB_HEAD_DIM).transpose(2, 0, 1, 3, 4)
    key_pos = jnp.arange(S)

    def block(args):
        q_blk, blk = args
        z = jnp.einsum('bhtd,bhsd->bhts', q_blk, kh) * scale
        q_pos = blk * SB_BLOCK + jnp.arange(SB_BLOCK)
        before = (key_pos[None, :] < q_pos[:, None])[None, None]
        log_beta = jax.nn.log_sigmoid(z)
        log_one_minus = jnp.where(before, jax.nn.log_sigmoid(-z), 0.0)
        between = lax.cumsum(log_one_minus, axis=3, reverse=True) - log_one_minus
        weights = jnp.exp(jnp.where(before, log_beta + between, -jnp.inf))
        return jnp.einsum('bhts,bhsd->bhtd', weights, vh)

    o = lax.map(block, (qb, jnp.arange(n_blk)))
    o = o.transpose(1, 0, 3, 2, 4).reshape(B, S, SB_HEADS, SB_HEAD_DIM)
    return head_rmsnorm(o, gnorm).reshape(B, S, SB_WIDTH)


def memory_cross_attention(h, m, w_q, w_kv, w_o):
    B, S, _ = h.shape
    q = (h @ w_q).reshape(B, S, MEM_HEADS, MEM_HEAD_DIM)
    k, v = jnp.split(m @ w_kv, 2, axis=-1)
    k = k.reshape(B, N_MEM, MEM_HEADS, MEM_HEAD_DIM)
    v = v.reshape(B, N_MEM, MEM_HEADS, MEM_HEAD_DIM)
    s = jnp.einsum('bthd,bnhd->bhtn', q, k).astype(jnp.float32) * (MEM_HEAD_DIM ** -0.5)
    p = jax.nn.softmax(s, axis=-1).astype(h.dtype)
    o = jnp.einsum('bhtn,bnhd->bthd', p, v).reshape(B, S, D_MODEL)
    return o @ w_o


def setup_inputs(seed: int = 0) -> dict:
    key = jax.random.key(seed)
    ks = jax.random.split(key, 24)
    f32 = jnp.float32

    def w(k, shape, fan_in, gain=1.0):
        return (jax.random.normal(k, shape, f32) * (gain * fan_in ** -0.5)).astype(f32)

    def norm_gain(k, shape):
        return (1.0 + 0.02 * jax.random.normal(k, shape, f32)).astype(f32)

    return {
        "x": jax.random.normal(ks[0], (BATCH, SEQ, D_MODEL), f32),
        "mem": jax.random.normal(ks[1], (BATCH, N_MEM, D_MODEL), f32),
        "ffn1_norm": norm_gain(ks[2], (DEPTH, D_MODEL)),
        "ffn1_w_gu": w(ks[3], (DEPTH, D_MODEL, 2 * D_FF), D_MODEL),
        "ffn1_w_down": w(ks[4], (DEPTH, D_FF, D_MODEL), D_FF),
        "mix_norm": norm_gain(ks[5], (DEPTH, D_MODEL)),
        "w_in": w(ks[6], (DEPTH, D_MODEL, IN_COLS), D_MODEL),
        "hg_lb_raw": (0.5 * jax.random.normal(ks[7], (DEPTH + 1, HG_WIDTH), f32)).astype(f32),
        "hg_gnorm": norm_gain(ks[8], (DEPTH, HG_WIDTH)),
        "sb_gnorm": norm_gain(ks[9], (DEPTH, SB_WIDTH)),
        "w_out": w(ks[10], (DEPTH, MIX_WIDTH, D_MODEL), MIX_WIDTH),
        "mem_q_norm": norm_gain(ks[11], (DEPTH, D_MODEL)),
        "mem_kv_norm": norm_gain(ks[12], (DEPTH, D_MODEL)),
        "mem_w_q": w(ks[13], (DEPTH, D_MODEL, D_MODEL), D_MODEL),
        "mem_w_kv": w(ks[14], (DEPTH, D_MODEL, 2 * D_MODEL), D_MODEL),
        "mem_w_o": w(ks[15], (DEPTH, D_MODEL, D_MODEL), D_MODEL),
        "ffn2_norm": norm_gain(ks[16], (DEPTH, D_MODEL)),
        "ffn2_w_gu": w(ks[17], (DEPTH, D_MODEL, 2 * D_FF), D_MODEL),
        "ffn2_w_down": w(ks[18], (DEPTH, D_FF, D_MODEL), D_FF),
        "final_norm": norm_gain(ks[19], (D_MODEL,)),
    }


def reference(x, mem, ffn1_norm, ffn1_w_gu, ffn1_w_down, mix_norm, w_in, hg_lb_raw,
              hg_gnorm, sb_gnorm, w_out, mem_q_norm, mem_kv_norm, mem_w_q, mem_w_kv,
              mem_w_o, ffn2_norm, ffn2_w_gu, ffn2_w_down, final_norm):
    lb_all = jnp.cumsum(jax.nn.softmax(hg_lb_raw.astype(jnp.float32), axis=0), axis=0)
    for l in range(DEPTH):
        x = x + 0.5 * swiglu(rmsnorm(x, ffn1_norm[l]), ffn1_w_gu[l], ffn1_w_down[l])

        h = rmsnorm(x, mix_norm[l])
        proj = h @ w_in[l]
        hg_q, hg_f, hg_i, hg_g, sb_q, sb_k, sb_v = jnp.split(
            proj, np.cumsum([HG_WIDTH] * 4 + [SB_WIDTH] * 2).tolist(), axis=-1)
        o_hg = hgrn2_mixer(hg_q, hg_f, hg_i, hg_g, lb_all[l], hg_gnorm[l])
        o_sb = stick_breaking_mixer(sb_q, sb_k, sb_v, sb_gnorm[l])
        mixed = jnp.concatenate([o_hg, o_sb], axis=-1).astype(x.dtype)
        x = x + mixed @ w_out[l]

        x = x + memory_cross_attention(rmsnorm(x, mem_q_norm[l]), rmsnorm(mem, mem_kv_norm[l]),
                                       mem_w_q[l], mem_w_kv[l], mem_w_o[l])

        x = x + 0.5 * swiglu(rmsnorm(x, ffn2_norm[l]), ffn2_w_gu[l], ffn2_w_down[l])
    return rmsnorm(x, final_norm)
```

```python
import functools

import jax
import jax.numpy as jnp
from jax import lax
from jax.experimental import pallas as pl
from jax.experimental.pallas import tpu as pltpu

F32 = jnp.float32
BF16 = jnp.bfloat16

EPS = 1e-6
CHUNK = 64
SUB = 16
HG_HEADS = 4
SB_HEADS = 8
MEM_HEADS = 4
SB_LOG_ZERO = -104.0

V7X_VMEM_LIMIT = 56 * 1024 * 1024


def _params(*sem):
    return pltpu.CompilerParams(dimension_semantics=sem, vmem_limit_bytes=V7X_VMEM_LIMIT)


def _resident(shape):
    nd = len(shape)
    return pl.BlockSpec(shape, lambda *_: (0,) * nd, pipeline_mode=pl.Buffered(1))


def _rmsnorm(x, g):
    return x * lax.rsqrt(jnp.mean(x * x, axis=-1, keepdims=True) + EPS) * g


def _dot(a, b):
    return jnp.dot(a, b, preferred_element_type=F32)


def _dot_nt(a, b):
    return lax.dot_general(a, b, (((1,), (1,)), ((), ())), preferred_element_type=F32)


def _split3(x):
    hi = x.astype(BF16)
    r = x - hi.astype(F32)
    mid = r.astype(BF16)
    lo = (r - mid.astype(F32)).astype(BF16)
    return hi, mid, lo


def _ffn_kernel(x_ref, g_ref, wgu_ref, wd_ref, *rest, d_ff, ck, final):
    if final:
        fg_ref, o_ref, act_ref = rest
    else:
        o_ref, act_ref = rest
    x = x_ref[...]
    h = _rmsnorm(x, g_ref[...]).astype(BF16)
    for c in range(d_ff // ck):
        gate = _dot(h, wgu_ref[:, c * ck:(c + 1) * ck])
        up = _dot(h, wgu_ref[:, d_ff + c * ck:d_ff + (c + 1) * ck])
        silu = gate / (1.0 + jnp.exp(-gate))
        act_ref[:, c * ck:(c + 1) * ck] = (silu * up).astype(BF16)
    y = _dot(act_ref[...], wd_ref[...])
    out = x + 0.5 * y
    if final:
        out = _rmsnorm(out, fg_ref[...])
    o_ref[...] = out


def _ffn(x, norm_g, w_gu, w_down, final_g=None, *, tm=512, ck=256):
    n, d = x.shape
    d_ff = w_down.shape[0]
    final = final_g is not None
    in_specs = [
        pl.BlockSpec((tm, d), lambda i: (i, 0)),
        _resident((1, d)),
        _resident((d, 2 * d_ff)),
        _resident((d_ff, d)),
    ]
    args = [x, norm_g.reshape(1, d), w_gu, w_down]
    if final:
        in_specs.append(_resident((1, d)))
        args.append(final_g.reshape(1, d))
    return pl.pallas_call(
        functools.partial(_ffn_kernel, d_ff=d_ff, ck=ck, final=final),
        grid=(n // tm,),
        in_specs=in_specs,
        out_specs=pl.BlockSpec((tm, d), lambda i: (i, 0)),
        out_shape=jax.ShapeDtypeStruct((n, d), F32),
        scratch_shapes=[pltpu.VMEM((tm, d_ff), BF16)],
        compiler_params=_params("parallel"),
        name="ffn_final" if final else "ffn",
    )(*args)


def _inproj_kernel(x_ref, g_ref, w_ref, hg_ref, sb_ref, *, hg_cols, sb_w, sb_scale, ck):
    h = _rmsnorm(x_ref[...], g_ref[...]).astype(BF16)
    for c in range(hg_cols // ck):
        hg_ref[:, c * ck:(c + 1) * ck] = _dot(h, w_ref[:, c * ck:(c + 1) * ck])
    for c in range(3 * sb_w // ck):
        p = _dot(h, w_ref[:, hg_cols + c * ck:hg_cols + (c + 1) * ck])
        if (c + 1) * ck <= sb_w:
            p = p * sb_scale
        sb_ref[:, c * ck:(c + 1) * ck] = p.astype(BF16)


def _in_proj(x, norm_g, w_in, *, hg_cols, sb_w, sb_scale, tm=512, ck=512):
    n, d = x.shape
    return pl.pallas_call(
        functools.partial(_inproj_kernel, hg_cols=hg_cols, sb_w=sb_w, sb_scale=sb_scale, ck=ck),
        grid=(n // tm,),
        in_specs=[
            pl.BlockSpec((tm, d), lambda i: (i, 0)),
            _resident((1, d)),
            _resident(w_in.shape),
        ],
        out_specs=[
            pl.BlockSpec((tm, hg_cols), lambda i: (i, 0)),
            pl.BlockSpec((tm, 3 * sb_w), lambda i: (i, 0)),
        ],
        out_shape=[
            jax.ShapeDtypeStruct((n, hg_cols), F32),
            jax.ShapeDtypeStruct((n, 3 * sb_w), BF16),
        ],
        compiler_params=_params("parallel"),
        name="in_proj",
    )(x, norm_g.reshape(1, d), w_in)


def _hgrn_kernel(q_ref, f_ref, i_ref, g_ref, lbraw_ref, gn_ref, tri_ref, o_ref, st_ref,
                 *, rows, hd, layer):
    @pl.when(pl.program_id(1) == 0)
    def _():
        st_ref[...] = jnp.zeros_like(st_ref)

    raw = lbraw_ref[...]
    e = jnp.exp(raw - jnp.max(raw, axis=0, keepdims=True))
    lb = jnp.sum(e[0:layer + 1, :], axis=0, keepdims=True) / jnp.sum(e, axis=0, keepdims=True)
    gn = gn_ref[...]
    tri = tri_ref[...]
    n_sub = CHUNK // SUB
    lane = lax.broadcasted_iota(jnp.int32, (SUB, CHUNK), 1)
    row = lax.broadcasted_iota(jnp.int32, (SUB, CHUNK), 0)

    def chunk_body(c, carry):
        r0 = pl.multiple_of(c * CHUNK, CHUNK)
        fl = f_ref[pl.ds(r0, CHUNK), :]
        f = lb + (1.0 - lb) / (1.0 + jnp.exp(-fl))
        kk = 1.0 - f
        logf = jnp.log(f)
        hi, mid, lo = _split3(logf)
        b = _dot(tri, hi) + _dot(tri, mid) + _dot(tri, lo)
        q = q_ref[pl.ds(r0, CHUNK), :]
        v = i_ref[pl.ds(r0, CHUNK), :]
        gate = g_ref[pl.ds(r0, CHUNK), :]
        b_last = b[CHUNK - 1:CHUNK, :]
        q_in = (q * jnp.exp(b)).astype(BF16)
        k_dec = (kk * jnp.exp(b_last - b)).astype(BF16)
        e_last = jnp.exp(b_last)
        v16 = v.astype(BF16)
        outs = []
        for h in range(q.shape[1] // hd):
            sl = slice(h * hd, (h + 1) * hd)
            qh, kh, bh, vh = q[:, sl], kk[:, sl], b[:, sl], v16[:, sl]
            vht = v[:, sl].T.astype(BF16)
            st = st_ref[h]
            o_inter = _dot_nt(q_in[:, sl], st.astype(BF16))
            a_rows = []
            for i in range(n_sub):
                rs = slice(i * SUB, (i + 1) * SUB)
                qi, ki, bi = qh[rs], kh[rs], bh[rs]
                dwide = jnp.zeros((SUB, CHUNK), F32)
                for s in range(SUB):
                    p = qi * (ki[s:s + 1] * jnp.exp(jnp.minimum(bi - bi[s:s + 1], 0.0)))
                    col = jnp.sum(p, axis=-1, keepdims=True)
                    dwide = jnp.where(lane == i * SUB + s, col, dwide)
                a_i = jnp.where(lane <= i * SUB + row, dwide, 0.0)
                if i > 0:
                    r_i = bh[i * SUB - 1:i * SUB]
                    qt = (qi * jnp.exp(bi - r_i)).astype(BF16)
                    kt = (kh * jnp.exp(jnp.minimum(r_i - bh, 0.0))).astype(BF16)
                    a_i = jnp.where(lane < i * SUB, _dot_nt(qt, kt), a_i)
                a_rows.append(a_i)
            a = jnp.concatenate(a_rows, axis=0).astype(BF16)
            o = _dot(a, vh) + o_inter
            st_ref[h] = st * e_last[:, sl] + _dot(vht, k_dec[:, sl])
            y = o * lax.rsqrt(jnp.mean(o * o, axis=-1, keepdims=True) + EPS)
            outs.append(y)
        y = jnp.concatenate(outs, axis=-1) * gn
        o_ref[pl.ds(r0, CHUNK), :] = (y * (gate / (1.0 + jnp.exp(-gate)))).astype(o_ref.dtype)
        return carry

    lax.fori_loop(0, rows // CHUNK, chunk_body, 0)


def _hgrn(hg, lb_raw, gnorm, *, batch, seq, width, layer, rows=512):
    n = batch * seq
    nblk = seq // rows
    hd = width // HG_HEADS
    tri = jnp.tril(jnp.ones((CHUNK, CHUNK), BF16))

    def col_spec(j):
        return pl.BlockSpec((rows, width), lambda b, t, j=j: (b * nblk + t, j))

    return pl.pallas_call(
        functools.partial(_hgrn_kernel, rows=rows, hd=hd, layer=layer),
        grid=(batch, nblk),
        in_specs=[col_spec(0), col_spec(1), col_spec(2), col_spec(3),
                  _resident(lb_raw.shape), _resident((1, width)), _resident((CHUNK, CHUNK))],
        out_specs=pl.BlockSpec((rows, width), lambda b, t: (b * nblk + t, 0)),
        out_shape=jax.ShapeDtypeStruct((n, width), BF16),
        scratch_shapes=[pltpu.VMEM((HG_HEADS, hd, hd), F32)],
        compiler_params=_params("parallel", "arbitrary"),
        name="hgrn",
    )(hg, hg, hg, hg, lb_raw, gnorm.reshape(1, width), tri)


def _sb_kernel(q_ref, k_ref, v_ref, gn_ref, u_ref, o_ref, acc_ref, rb_ref, *, tq, hd):
    qi = pl.program_id(2)
    q = q_ref[...]
    u = u_ref[...]
    lane = lax.broadcasted_iota(jnp.int32, q.shape, 1)
    rowi = lax.broadcasted_iota(jnp.int32, (tq, tq), 0)
    coli = lax.broadcasted_iota(jnp.int32, (tq, tq), 1)
    strictly_before = coli < rowi

    def block(qm, kb, rb, masked):
        r0 = pl.multiple_of(kb * tq, tq)
        k = k_ref[pl.ds(r0, tq), :]
        v = v_ref[pl.ds(r0, tq), :]
        z = _dot_nt(qm, k)
        lom = -(jnp.maximum(z, 0.0) + jnp.log(1.0 + jnp.exp(-jnp.abs(z))))
        if masked:
            lom = jnp.where(strictly_before, lom, 0.0)
        hi = lom.astype(BF16)
        lo = (lom - hi.astype(F32)).astype(BF16)
        c = _dot(hi, u) + _dot(lo, u)
        if rb is not None:
            c = c + rb
        w = jnp.exp(z + c)
        if masked:
            w = jnp.where(strictly_before, w, 0.0)
        return _dot(w.astype(BF16), v), c

    for h in range(2):
        qm = jnp.where((lane >= h * hd) & (lane < (h + 1) * hd), q, jnp.zeros_like(q))
        pv, c = block(qm, qi, None, True)
        acc_ref[h] = pv
        rb_ref[...] = jnp.broadcast_to(c[:, 0:1], (tq, tq))

        def cond(carry):
            kb, live = carry
            return (kb >= 0) & (live > 0)

        def body(carry, qm=qm, h=h):
            kb, _ = carry
            pv, c = block(qm, kb, rb_ref[...], False)
            acc_ref[h] += pv
            rem = c[:, 0:1]
            rb_ref[...] = jnp.broadcast_to(rem, (tq, tq))
            live = (jnp.max(rem) >= SB_LOG_ZERO).astype(jnp.int32)
            return kb - 1, live

        live0 = (jnp.max(c[:, 0:1]) >= SB_LOG_ZERO).astype(jnp.int32)
        lax.while_loop(cond, body, (qi - 1, live0))

    o = jnp.where(lane < hd, acc_ref[0], acc_ref[1])
    o2 = o * o
    ms0 = jnp.sum(jnp.where(lane < hd, o2, 0.0), axis=-1, keepdims=True)
    ms1 = jnp.sum(jnp.where(lane < hd, 0.0, o2), axis=-1, keepdims=True)
    ms = jnp.where(lane < hd, ms0, ms1) * (1.0 / hd)
    o_ref[...] = (o * lax.rsqrt(ms + EPS) * gn_ref[...]).astype(o_ref.dtype)


def _sb(sbp, gnorm, *, batch, seq, width, tq=256):
    n = batch * seq
    hd = width // SB_HEADS
    pair = 2 * hd
    n_pair = width // pair
    nq = seq // tq
    u = (jnp.arange(tq)[:, None] >= jnp.arange(tq)[None, :]).astype(BF16)
    return pl.pallas_call(
        functools.partial(_sb_kernel, tq=tq, hd=hd),
        grid=(batch, n_pair, nq),
        in_specs=[
            pl.BlockSpec((tq, pair), lambda b, p, t: (b * nq + t, p)),
            pl.BlockSpec((seq, pair), lambda b, p, t: (b, n_pair + p)),
            pl.BlockSpec((seq, pair), lambda b, p, t: (b, 2 * n_pair + p)),
            pl.BlockSpec((1, pair), lambda b, p, t: (0, p)),
            _resident((tq, tq)),
        ],
        out_specs=pl.BlockSpec((tq, pair), lambda b, p, t: (b * nq + t, p)),
        out_shape=jax.ShapeDtypeStruct((n, width), BF16),
        scratch_shapes=[pltpu.VMEM((2, tq, pair), F32), pltpu.VMEM((tq, tq), F32)],
        compiler_params=_params("parallel", "parallel", "arbitrary"),
        name="sb",
    )(sbp, sbp, sbp, gnorm.reshape(1, width), u)


def _memkv_kernel(m_ref, g_ref, w_ref, o_ref):
    h = _rmsnorm(m_ref[...], g_ref[...]).astype(BF16)
    o_ref[...] = _dot(h, w_ref[...]).astype(o_ref.dtype)


def _mem_kv(mem2d, norm_g, w_kv, *, tm=256):
    n, d = mem2d.shape
    return pl.pallas_call(
        _memkv_kernel,
        grid=(n // tm,),
        in_specs=[pl.BlockSpec((tm, d), lambda i: (i, 0)), _resident((1, d)), _resident(w_kv.shape)],
        out_specs=pl.BlockSpec((tm, w_kv.shape[1]), lambda i: (i, 0)),
        out_shape=jax.ShapeDtypeStruct((n, w_kv.shape[1]), BF16),
        compiler_params=_params("parallel"),
        name="mem_kv",
    )(mem2d, norm_g.reshape(1, d), w_kv)


def _post_kernel(x_ref, ohg_ref, osb_ref, wout_ref, qg_ref, wq_ref, kv_ref, wo_ref, o_ref, att_ref,
                 *, hg_w, heads):
    d = x_ref.shape[1]
    dh = d // heads
    x = x_ref[...] + _dot(ohg_ref[...], wout_ref[:hg_w, :]) + _dot(osb_ref[...], wout_ref[hg_w:, :])
    hq = _rmsnorm(x, qg_ref[...]).astype(BF16)
    q = (_dot(hq, wq_ref[...]) * (dh ** -0.5)).astype(BF16)
    for h in range(heads):
        k = kv_ref[:, h * dh:(h + 1) * dh]
        v = kv_ref[:, d + h * dh:d + (h + 1) * dh]
        s = _dot_nt(q[:, h * dh:(h + 1) * dh], k)
        e = jnp.exp(s - jnp.max(s, axis=-1, keepdims=True))
        p = e / jnp.sum(e, axis=-1, keepdims=True)
        att_ref[:, h * dh:(h + 1) * dh] = _dot(p.astype(BF16), v).astype(BF16)
    o_ref[...] = x + _dot(att_ref[...], wo_ref[...])


def _post(x, o_hg, o_sb, w_out, q_norm, w_q, kv, w_o, *, seq, n_mem, tm=512):
    n, d = x.shape
    hg_w = o_hg.shape[1]
    per_batch = seq // tm
    return pl.pallas_call(
        functools.partial(_post_kernel, hg_w=hg_w, heads=MEM_HEADS),
        grid=(n // tm,),
        in_specs=[
            pl.BlockSpec((tm, d), lambda i: (i, 0)),
            pl.BlockSpec((tm, hg_w), lambda i: (i, 0)),
            pl.BlockSpec((tm, o_sb.shape[1]), lambda i: (i, 0)),
            _resident(w_out.shape),
            _resident((1, d)),
            _resident(w_q.shape),
            pl.BlockSpec((n_mem, 2 * d), lambda i: (i // per_batch, 0)),
            _resident(w_o.shape),
        ],
        out_specs=pl.BlockSpec((tm, d), lambda i: (i, 0)),
        out_shape=jax.ShapeDtypeStruct((n, d), F32),
        scratch_shapes=[pltpu.VMEM((tm, d), BF16)],
        compiler_params=_params("parallel"),
        name="post",
    )(x, o_hg, o_sb, w_out, q_norm.reshape(1, d), w_q, kv, w_o)


def kernel(x, mem, ffn1_norm, ffn1_w_gu, ffn1_w_down, mix_norm, w_in, hg_lb_raw, hg_gnorm, sb_gnorm,
           w_out, mem_q_norm, mem_kv_norm, mem_w_q, mem_w_kv, mem_w_o, ffn2_norm, ffn2_w_gu,
           ffn2_w_down, final_norm):
    batch, seq, d = x.shape
    n_mem = mem.shape[1]
    depth = ffn1_norm.shape[0]
    hg_w = hg_gnorm.shape[1]
    sb_w = sb_gnorm.shape[1]
    sb_scale = (sb_w // SB_HEADS) ** -0.5

    xs = x.reshape(batch * seq, d)
    mem2d = mem.reshape(batch * n_mem, d)
    for l in range(depth):
        xs = _ffn(xs, ffn1_norm[l], ffn1_w_gu[l].astype(BF16), ffn1_w_down[l].astype(BF16))
        hg, sbp = _in_proj(xs, mix_norm[l], w_in[l].astype(BF16),
                           hg_cols=4 * hg_w, sb_w=sb_w, sb_scale=sb_scale)
        o_hg = _hgrn(hg, hg_lb_raw, hg_gnorm[l], batch=batch, seq=seq, width=hg_w, layer=l)
        o_sb = _sb(sbp, sb_gnorm[l], batch=batch, seq=seq, width=sb_w)
        kv = _mem_kv(mem2d, mem_kv_norm[l], mem_w_kv[l].astype(BF16))
        xs = _post(xs, o_hg, o_sb, w_out[l].astype(BF16), mem_q_norm[l], mem_w_q[l].astype(BF16),
                   kv, mem_w_o[l].astype(BF16), seq=seq, n_mem=n_mem)
        last = l == depth - 1
        xs = _ffn(xs, ffn2_norm[l], ffn2_w_gu[l].astype(BF16), ffn2_w_down[l].astype(BF16),
                  final_norm if last else None)
    return xs.reshape(batch, seq, d)
```

```python
import functools

import jax
import jax.numpy as jnp
from jax import lax
from jax.experimental import pallas as pl
from jax.experimental.pallas import tpu as pltpu

F32 = jnp.float32
BF16 = jnp.bfloat16

EPS = 1e-6
CHUNK = 64
SUB = 16
HG_HEADS = 4
SB_HEADS = 8
MEM_HEADS = 4
HG_SAFE_DECAY = 80.0
SB_LOG_ZERO = -104.0

V7X_VMEM_LIMIT = 56 * 1024 * 1024


def _params(*sem):
    return pltpu.CompilerParams(dimension_semantics=sem, vmem_limit_bytes=V7X_VMEM_LIMIT)


def _resident(shape):
    nd = len(shape)
    return pl.BlockSpec(shape, lambda *_: (0,) * nd, pipeline_mode=pl.Buffered(1))


def _rmsnorm(x, g):
    return x * lax.rsqrt(jnp.mean(x * x, axis=-1, keepdims=True) + EPS) * g


def _dot(a, b):
    return jnp.dot(a, b, preferred_element_type=F32)


def _dot_nt(a, b):
    return lax.dot_general(a, b, (((1,), (1,)), ((), ())), preferred_element_type=F32)


def _split3(x):
    hi = x.astype(BF16)
    r = x - hi.astype(F32)
    mid = r.astype(BF16)
    lo = (r - mid.astype(F32)).astype(BF16)
    return hi, mid, lo


def _ffn_kernel(x_ref, g_ref, wgu_ref, wd_ref, *rest, d_ff, ck, final):
    if final:
        fg_ref, o_ref, act_ref = rest
    else:
        o_ref, act_ref = rest
    x = x_ref[...]
    h = _rmsnorm(x, g_ref[...]).astype(BF16)
    for c in range(d_ff // ck):
        gate = _dot(h, wgu_ref[:, c * ck:(c + 1) * ck])
        up = _dot(h, wgu_ref[:, d_ff + c * ck:d_ff + (c + 1) * ck])
        silu = gate / (1.0 + jnp.exp(-gate))
        act_ref[:, c * ck:(c + 1) * ck] = (silu * up).astype(BF16)
    y = _dot(act_ref[...], wd_ref[...])
    out = x + 0.5 * y
    if final:
        out = _rmsnorm(out, fg_ref[...])
    o_ref[...] = out


def _ffn(x, norm_g, w_gu, w_down, final_g=None, *, tm=512, ck=256):
    n, d = x.shape
    d_ff = w_down.shape[0]
    final = final_g is not None
    in_specs = [
        pl.BlockSpec((tm, d), lambda i: (i, 0)),
        _resident((1, d)),
        _resident((d, 2 * d_ff)),
        _resident((d_ff, d)),
    ]
    args = [x, norm_g.reshape(1, d), w_gu, w_down]
    if final:
        in_specs.append(_resident((1, d)))
        args.append(final_g.reshape(1, d))
    return pl.pallas_call(
        functools.partial(_ffn_kernel, d_ff=d_ff, ck=ck, final=final),
        grid=(n // tm,),
        in_specs=in_specs,
        out_specs=pl.BlockSpec((tm, d), lambda i: (i, 0)),
        out_shape=jax.ShapeDtypeStruct((n, d), F32),
        scratch_shapes=[pltpu.VMEM((tm, d_ff), BF16)],
        compiler_params=_params("parallel"),
        name="ffn_final" if final else "ffn",
    )(*args)


def _inproj_kernel(x_ref, g_ref, w_ref, hg_ref, sb_ref, *, hg_cols, sb_w, sb_scale, ck):
    h = _rmsnorm(x_ref[...], g_ref[...]).astype(BF16)
    for c in range(hg_cols // ck):
        hg_ref[:, c * ck:(c + 1) * ck] = _dot(h, w_ref[:, c * ck:(c + 1) * ck])
    for c in range(3 * sb_w // ck):
        p = _dot(h, w_ref[:, hg_cols + c * ck:hg_cols + (c + 1) * ck])
        if (c + 1) * ck <= sb_w:
            p = p * sb_scale
        sb_ref[:, c * ck:(c + 1) * ck] = p.astype(BF16)


def _in_proj(x, norm_g, w_in, *, hg_cols, sb_w, sb_scale, tm=512, ck=512):
    n, d = x.shape
    return pl.pallas_call(
        functools.partial(_inproj_kernel, hg_cols=hg_cols, sb_w=sb_w, sb_scale=sb_scale, ck=ck),
        grid=(n // tm,),
        in_specs=[
            pl.BlockSpec((tm, d), lambda i: (i, 0)),
            _resident((1, d)),
            _resident(w_in.shape),
        ],
        out_specs=[
            pl.BlockSpec((tm, hg_cols), lambda i: (i, 0)),
            pl.BlockSpec((tm, 3 * sb_w), lambda i: (i, 0)),
        ],
        out_shape=[
            jax.ShapeDtypeStruct((n, hg_cols), F32),
            jax.ShapeDtypeStruct((n, 3 * sb_w), BF16),
        ],
        compiler_params=_params("parallel"),
        name="in_proj",
    )(x, norm_g.reshape(1, d), w_in)


def _hgrn_kernel(q_ref, f_ref, i_ref, g_ref, lbraw_ref, gn_ref, tri_ref, o_ref, st_ref, b_ref, kk_ref,
                 *, rows, hd, layer):
    @pl.when(pl.program_id(1) == 0)
    def _():
        st_ref[...] = jnp.zeros_like(st_ref)

    raw = lbraw_ref[...]
    e = jnp.exp(raw - jnp.max(raw, axis=0, keepdims=True))
    lb = jnp.sum(e[0:layer + 1, :], axis=0, keepdims=True) / jnp.sum(e, axis=0, keepdims=True)
    gn = gn_ref[...]
    tri = tri_ref[...]
    n_chunks = rows // CHUNK
    n_heads = q_ref.shape[1] // hd
    n_sub = CHUNK // SUB

    f = lb + (1.0 - lb) / (1.0 + jnp.exp(-f_ref[...]))
    kk_ref[...] = 1.0 - f
    hi, mid, lo = _split3(jnp.log(f))
    for c in range(n_chunks):
        rs = slice(c * CHUNK, (c + 1) * CHUNK)
        b_ref[rs, :] = _dot(tri, hi[rs]) + _dot(tri, mid[rs]) + _dot(tri, lo[rs])
    b_min = jnp.min(b_ref[...])

    def finish(r0, outs, gate):
        y = jnp.concatenate(outs, axis=-1) * gn
        o_ref[pl.ds(r0, CHUNK), :] = (y * (gate / (1.0 + jnp.exp(-gate)))).astype(o_ref.dtype)

    def head_norm(o):
        return o * lax.rsqrt(jnp.mean(o * o, axis=-1, keepdims=True) + EPS)

    @pl.when(b_min >= -HG_SAFE_DECAY)
    def _():
        tril = (lax.broadcasted_iota(jnp.int32, (CHUNK, CHUNK), 1)
                <= lax.broadcasted_iota(jnp.int32, (CHUNK, CHUNK), 0))
        for c in range(n_chunks):
            r0 = c * CHUNK
            rs = slice(r0, r0 + CHUNK)
            b, kk = b_ref[rs, :], kk_ref[rs, :]
            q, v, gate = q_ref[rs, :], i_ref[rs, :], g_ref[rs, :]
            e_last = jnp.exp(b[CHUNK - 1:CHUNK, :])
            q_in = (q * jnp.exp(b)).astype(BF16)
            kt = kk * jnp.exp(-b)
            k_dec = (kt * e_last).astype(BF16)
            kt = kt.astype(BF16)
            outs = []
            for h in range(n_heads):
                sl = slice(h * hd, (h + 1) * hd)
                a = jnp.where(tril, _dot_nt(q_in[:, sl], kt[:, sl]), 0.0).astype(BF16)
                vt = v[:, sl].T.astype(BF16)
                st = st_ref[h]
                o = _dot_nt(jnp.concatenate([q_in[:, sl], a], axis=1),
                            jnp.concatenate([st.astype(BF16), vt], axis=1))
                st_ref[h] = st * e_last[:, sl] + _dot(vt, k_dec[:, sl])
                outs.append(head_norm(o))
            finish(r0, outs, gate)

    @pl.when(b_min < -HG_SAFE_DECAY)
    def _():
        lane = lax.broadcasted_iota(jnp.int32, (SUB, CHUNK), 1)
        row = lax.broadcasted_iota(jnp.int32, (SUB, CHUNK), 0)

        def chunk_body(c, carry):
            r0 = pl.multiple_of(c * CHUNK, CHUNK)
            b = b_ref[pl.ds(r0, CHUNK), :]
            kk = kk_ref[pl.ds(r0, CHUNK), :]
            q = q_ref[pl.ds(r0, CHUNK), :]
            v = i_ref[pl.ds(r0, CHUNK), :]
            gate = g_ref[pl.ds(r0, CHUNK), :]
            b_last = b[CHUNK - 1:CHUNK, :]
            q_in = (q * jnp.exp(b)).astype(BF16)
            k_dec = (kk * jnp.exp(b_last - b)).astype(BF16)
            e_last = jnp.exp(b_last)
            v16 = v.astype(BF16)
            outs = []
            for h in range(n_heads):
                sl = slice(h * hd, (h + 1) * hd)
                qh, kh, bh, vh = q[:, sl], kk[:, sl], b[:, sl], v16[:, sl]
                vt = v[:, sl].T.astype(BF16)
                st = st_ref[h]
                o_inter = _dot_nt(q_in[:, sl], st.astype(BF16))
                a_rows = []
                for i in range(n_sub):
                    rs = slice(i * SUB, (i + 1) * SUB)
                    qi, ki, bi = qh[rs], kh[rs], bh[rs]
                    dwide = jnp.zeros((SUB, CHUNK), F32)
                    for s in range(SUB):
                        p = qi * (ki[s:s + 1] * jnp.exp(jnp.minimum(bi - bi[s:s + 1], 0.0)))
                        col = jnp.sum(p, axis=-1, keepdims=True)
                        dwide = jnp.where(lane == i * SUB + s, col, dwide)
                    a_i = jnp.where(lane <= i * SUB + row, dwide, 0.0)
                    if i > 0:
                        r_i = bh[i * SUB - 1:i * SUB]
                        qt = (qi * jnp.exp(bi - r_i)).astype(BF16)
                        kt = (kh * jnp.exp(jnp.minimum(r_i - bh, 0.0))).astype(BF16)
                        a_i = jnp.where(lane < i * SUB, _dot_nt(qt, kt), a_i)
                    a_rows.append(a_i)
                a = jnp.concatenate(a_rows, axis=0).astype(BF16)
                o = _dot(a, vh) + o_inter
                st_ref[h] = st * e_last[:, sl] + _dot(vt, k_dec[:, sl])
                outs.append(head_norm(o))
            finish(r0, outs, gate)
            return carry

        lax.fori_loop(0, n_chunks, chunk_body, 0)


def _hgrn(hg, lb_raw, gnorm, *, batch, seq, width, layer, rows=256):
    n = batch * seq
    nblk = seq // rows
    hd = width // HG_HEADS
    tri = jnp.tril(jnp.ones((CHUNK, CHUNK), BF16))

    def col_spec(j):
        return pl.BlockSpec((rows, width), lambda b, t, j=j: (b * nblk + t, j))

    return pl.pallas_call(
        functools.partial(_hgrn_kernel, rows=rows, hd=hd, layer=layer),
        grid=(batch, nblk),
        in_specs=[col_spec(0), col_spec(1), col_spec(2), col_spec(3),
                  _resident(lb_raw.shape), _resident((1, width)), _resident((CHUNK, CHUNK))],
        out_specs=pl.BlockSpec((rows, width), lambda b, t: (b * nblk + t, 0)),
        out_shape=jax.ShapeDtypeStruct((n, width), BF16),
        scratch_shapes=[pltpu.VMEM((HG_HEADS, hd, hd), F32),
                        pltpu.VMEM((rows, width), F32), pltpu.VMEM((rows, width), F32)],
        compiler_params=_params("parallel", "arbitrary"),
        name="hgrn",
    )(hg, hg, hg, hg, lb_raw, gnorm.reshape(1, width), tri)


def _sb_kernel(q_ref, k_ref, v_ref, gn_ref, uu_ref, o_ref, acc_ref, rem_ref, *, tq, hd, rs, win):
    n_sub = tq // rs
    t0 = pl.program_id(2) * tq
    q = q_ref[...]
    uu = uu_ref[...]
    lane = lax.broadcasted_iota(jnp.int32, (rs, 2 * hd), 1)
    col = lax.broadcasted_iota(jnp.int32, (rs, win), 1)
    col_minus_row = col - lax.broadcasted_iota(jnp.int32, (rs, win), 0)
    zero = jnp.zeros((rs, 2 * hd), q.dtype)

    def sweep(starts, limits, causal, first):
        zs = []
        for r in range(n_sub):
            qr = q[r * rs:(r + 1) * rs]
            qq = jnp.concatenate([jnp.where(lane < hd, qr, zero), jnp.where(lane < hd, zero, qr)], axis=0)
            z = _dot_nt(qq, k_ref[pl.ds(starts[r], win), :])
            valid = (col_minus_row if causal else col) < limits[r]
            zs.append(jnp.where(jnp.concatenate([valid, valid], axis=0), z, -1e30))
        z = jnp.concatenate(zs, axis=0)
        lom = -(jnp.maximum(z, 0.0) + jnp.log(1.0 + jnp.exp(-jnp.abs(z))))
        hi = lom.astype(BF16)
        lo = (lom - hi.astype(F32)).astype(BF16)
        c = _dot(jnp.concatenate([hi, lo], axis=1), uu)
        if not first:
            c = c + rem_ref[...]
        w = jnp.exp(z + c).astype(BF16)
        for r in range(n_sub):
            pv = _dot(w[2 * r * rs:2 * (r + 1) * rs], v_ref[pl.ds(starts[r], win), :])
            o_r = jnp.where(lane < hd, pv[:rs], pv[rs:])
            if first:
                acc_ref[r] = o_r
            else:
                acc_ref[r] += o_r
        rem = c[:, 0:1]
        rem_ref[...] = rem
        return jnp.max(rem)

    def frontier(r, it):
        return pl.multiple_of(jnp.maximum(t0 + (r + 1) * rs - (it + 1) * win, 0), rs)

    starts = [frontier(r, 0) for r in range(n_sub)]
    limits = [t0 + r * rs - starts[r] for r in range(n_sub)]
    rem_max = sweep(starts, limits, True, True)

    def cond(carry):
        it, live = carry
        return (frontier(n_sub - 1, it - 1) > 0) & (live > 0)

    def body(carry):
        it, _ = carry
        ends = [frontier(r, it - 1) for r in range(n_sub)]
        starts = [frontier(r, it) for r in range(n_sub)]
        limits = [ends[r] - starts[r] for r in range(n_sub)]
        rem_max = sweep(starts, limits, False, False)
        return it + 1, (rem_max >= SB_LOG_ZERO).astype(jnp.int32)

    lax.while_loop(cond, body, (jnp.int32(1), (rem_max >= SB_LOG_ZERO).astype(jnp.int32)))

    lane_q = lax.broadcasted_iota(jnp.int32, (tq, 2 * hd), 1)
    o = acc_ref[...].reshape(tq, 2 * hd)
    o2 = o * o
    ms0 = jnp.sum(jnp.where(lane_q < hd, o2, 0.0), axis=-1, keepdims=True)
    ms1 = jnp.sum(jnp.where(lane_q < hd, 0.0, o2), axis=-1, keepdims=True)
    ms = jnp.where(lane_q < hd, ms0, ms1) * (1.0 / hd)
    o_ref[...] = (o * lax.rsqrt(ms + EPS) * gn_ref[...]).astype(o_ref.dtype)


def _sb(sbp, gnorm, *, batch, seq, width, tq=256, rs=64, win=256):
    n = batch * seq
    hd = width // SB_HEADS
    pair = 2 * hd
    n_pair = width // pair
    nq = seq // tq
    n_sub = tq // rs
    u = (jnp.arange(win)[:, None] >= jnp.arange(win)[None, :]).astype(BF16)
    return pl.pallas_call(
        functools.partial(_sb_kernel, tq=tq, hd=hd, rs=rs, win=win),
        grid=(batch, n_pair, nq),
        in_specs=[
            pl.BlockSpec((tq, pair), lambda b, p, t: (b * nq + t, p)),
            pl.BlockSpec((seq, pair), lambda b, p, t: (b, n_pair + p)),
            pl.BlockSpec((seq, pair), lambda b, p, t: (b, 2 * n_pair + p)),
            pl.BlockSpec((1, pair), lambda b, p, t: (0, p)),
            _resident((2 * win, win)),
        ],
        out_specs=pl.BlockSpec((tq, pair), lambda b, p, t: (b * nq + t, p)),
        out_shape=jax.ShapeDtypeStruct((n, width), BF16),
        scratch_shapes=[pltpu.VMEM((n_sub, rs, pair), F32), pltpu.VMEM((n_sub * 2 * rs, 1), F32)],
        compiler_params=_params("parallel", "parallel", "arbitrary"),
        name="sb",
    )(sbp, sbp, sbp, gnorm.reshape(1, width), jnp.concatenate([u, u], axis=0))


def _memkv_kernel(m_ref, g_ref, w_ref, o_ref):
    h = _rmsnorm(m_ref[...], g_ref[...]).astype(BF16)
    o_ref[...] = _dot(h, w_ref[...]).astype(o_ref.dtype)


def _mem_kv(mem2d, norm_g, w_kv, *, tm=256):
    n, d = mem2d.shape
    return pl.pallas_call(
        _memkv_kernel,
        grid=(n // tm,),
        in_specs=[pl.BlockSpec((tm, d), lambda i: (i, 0)), _resident((1, d)), _resident(w_kv.shape)],
        out_specs=pl.BlockSpec((tm, w_kv.shape[1]), lambda i: (i, 0)),
        out_shape=jax.ShapeDtypeStruct((n, w_kv.shape[1]), BF16),
        compiler_params=_params("parallel"),
        name="mem_kv",
    )(mem2d, norm_g.reshape(1, d), w_kv)


def _post_kernel(x_ref, ohg_ref, osb_ref, wout_ref, qg_ref, wq_ref, kv_ref, wo_ref, o_ref, att_ref,
                 *, hg_w, heads):
    d = x_ref.shape[1]
    dh = d // heads
    x = x_ref[...] + _dot(ohg_ref[...], wout_ref[:hg_w, :]) + _dot(osb_ref[...], wout_ref[hg_w:, :])
    hq = _rmsnorm(x, qg_ref[...]).astype(BF16)
    q = (_dot(hq, wq_ref[...]) * (dh ** -0.5)).astype(BF16)
    for h in range(heads):
        k = kv_ref[:, h * dh:(h + 1) * dh]
        v = kv_ref[:, d + h * dh:d + (h + 1) * dh]
        s = _dot_nt(q[:, h * dh:(h + 1) * dh], k)
        e = jnp.exp(s - jnp.max(s, axis=-1, keepdims=True))
        p = e / jnp.sum(e, axis=-1, keepdims=True)
        att_ref[:, h * dh:(h + 1) * dh] = _dot(p.astype(BF16), v).astype(BF16)
    o_ref[...] = x + _dot(att_ref[...], wo_ref[...])


def _post(x, o_hg, o_sb, w_out, q_norm, w_q, kv, w_o, *, seq, n_mem, tm=512):
    n, d = x.shape
    hg_w = o_hg.shape[1]
    per_batch = seq // tm
    return pl.pallas_call(
        functools.partial(_post_kernel, hg_w=hg_w, heads=MEM_HEADS),
        grid=(n // tm,),
        in_specs=[
            pl.BlockSpec((tm, d), lambda i: (i, 0)),
            pl.BlockSpec((tm, hg_w), lambda i: (i, 0)),
            pl.BlockSpec((tm, o_sb.shape[1]), lambda i: (i, 0)),
            _resident(w_out.shape),
            _resident((1, d)),
            _resident(w_q.shape),
            pl.BlockSpec((n_mem, 2 * d), lambda i: (i // per_batch, 0)),
            _resident(w_o.shape),
        ],
        out_specs=pl.BlockSpec((tm, d), lambda i: (i, 0)),
        out_shape=jax.ShapeDtypeStruct((n, d), F32),
        scratch_shapes=[pltpu.VMEM((tm, d), BF16)],
        compiler_params=_params("parallel"),
        name="post",
    )(x, o_hg, o_sb, w_out, q_norm.reshape(1, d), w_q, kv, w_o)


def kernel(x, mem, ffn1_norm, ffn1_w_gu, ffn1_w_down, mix_norm, w_in, hg_lb_raw, hg_gnorm, sb_gnorm,
           w_out, mem_q_norm, mem_kv_norm, mem_w_q, mem_w_kv, mem_w_o, ffn2_norm, ffn2_w_gu,
           ffn2_w_down, final_norm):
    batch, seq, d = x.shape
    n_mem = mem.shape[1]
    depth = ffn1_norm.shape[0]
    hg_w = hg_gnorm.shape[1]
    sb_w = sb_gnorm.shape[1]
    sb_scale = (sb_w // SB_HEADS) ** -0.5

    xs = x.reshape(batch * seq, d)
    mem2d = mem.reshape(batch * n_mem, d)
    for l in range(depth):
        xs = _ffn(xs, ffn1_norm[l], ffn1_w_gu[l].astype(BF16), ffn1_w_down[l].astype(BF16))
        hg, sbp = _in_proj(xs, mix_norm[l], w_in[l].astype(BF16),
                           hg_cols=4 * hg_w, sb_w=sb_w, sb_scale=sb_scale)
        o_hg = _hgrn(hg, hg_lb_raw, hg_gnorm[l], batch=batch, seq=seq, width=hg_w, layer=l)
        o_sb = _sb(sbp, sb_gnorm[l], batch=batch, seq=seq, width=sb_w)
        kv = _mem_kv(mem2d, mem_kv_norm[l], mem_w_kv[l].astype(BF16))
        xs = _post(xs, o_hg, o_sb, w_out[l].astype(BF16), mem_q_norm[l], mem_w_q[l].astype(BF16),
                   kv, mem_w_o[l].astype(BF16), seq=seq, n_mem=n_mem)
        last = l == depth - 1
        xs = _ffn(xs, ffn2_norm[l], ffn2_w_gu[l].astype(BF16), ffn2_w_down[l].astype(BF16),
                  final_norm if last else None)
    return xs.reshape(batch, seq, d)
```

```python
import functools

import jax
import jax.numpy as jnp
from jax import lax
from jax.experimental import pallas as pl
from jax.experimental.pallas import tpu as pltpu

F32 = jnp.float32
BF16 = jnp.bfloat16

EPS = 1e-6
CHUNK = 64
SUB = 16
HG_HEADS = 4
SB_HEADS = 8
MEM_HEADS = 4
HG_SAFE_DECAY = 80.0
SB_LOG_ZERO = -104.0
LOG2E = 1.4426950408889634

V7X_VMEM_LIMIT = 56 * 1024 * 1024


def _params(*sem):
    return pltpu.CompilerParams(dimension_semantics=sem, vmem_limit_bytes=V7X_VMEM_LIMIT)


def _resident(shape):
    nd = len(shape)
    return pl.BlockSpec(shape, lambda *_: (0,) * nd, pipeline_mode=pl.Buffered(1))


def _rmsnorm(x, g):
    return x * lax.rsqrt(jnp.mean(x * x, axis=-1, keepdims=True) + EPS) * g


def _dot(a, b):
    return jnp.dot(a, b, preferred_element_type=F32)


def _dot_nt(a, b):
    return lax.dot_general(a, b, (((1,), (1,)), ((), ())), preferred_element_type=F32)


def _split3(x):
    hi = x.astype(BF16)
    r = x - hi.astype(F32)
    mid = r.astype(BF16)
    lo = (r - mid.astype(F32)).astype(BF16)
    return hi, mid, lo


def _ffn_kernel(x_ref, g_ref, wgu_ref, wd_ref, *rest, d_ff, ck, final):
    if final:
        fg_ref, o_ref, act_ref = rest
    else:
        o_ref, act_ref = rest
    x = x_ref[...]
    h = _rmsnorm(x, g_ref[...]).astype(BF16)
    for c in range(d_ff // ck):
        gate = _dot(h, wgu_ref[:, c * ck:(c + 1) * ck])
        up = _dot(h, wgu_ref[:, d_ff + c * ck:d_ff + (c + 1) * ck])
        silu = gate / (1.0 + jnp.exp(-gate))
        act_ref[:, c * ck:(c + 1) * ck] = (silu * up).astype(BF16)
    y = _dot(act_ref[...], wd_ref[...])
    out = x + 0.5 * y
    if final:
        out = _rmsnorm(out, fg_ref[...])
    o_ref[...] = out


def _ffn(x, norm_g, w_gu, w_down, final_g=None, *, tm=512, ck=256):
    n, d = x.shape
    d_ff = w_down.shape[0]
    final = final_g is not None
    in_specs = [
        pl.BlockSpec((tm, d), lambda i: (i, 0)),
        _resident((1, d)),
        _resident((d, 2 * d_ff)),
        _resident((d_ff, d)),
    ]
    args = [x, norm_g.reshape(1, d), w_gu, w_down]
    if final:
        in_specs.append(_resident((1, d)))
        args.append(final_g.reshape(1, d))
    return pl.pallas_call(
        functools.partial(_ffn_kernel, d_ff=d_ff, ck=ck, final=final),
        grid=(n // tm,),
        in_specs=in_specs,
        out_specs=pl.BlockSpec((tm, d), lambda i: (i, 0)),
        out_shape=jax.ShapeDtypeStruct((n, d), F32),
        scratch_shapes=[pltpu.VMEM((tm, d_ff), BF16)],
        compiler_params=_params("parallel"),
        name="ffn_final" if final else "ffn",
    )(*args)


def _inproj_kernel(x_ref, g_ref, w_ref, hg_ref, sb_ref, *, hg_cols, sb_w, sb_scale, ck):
    h = _rmsnorm(x_ref[...], g_ref[...]).astype(BF16)
    for c in range(hg_cols // ck):
        hg_ref[:, c * ck:(c + 1) * ck] = _dot(h, w_ref[:, c * ck:(c + 1) * ck])
    for c in range(3 * sb_w // ck):
        p = _dot(h, w_ref[:, hg_cols + c * ck:hg_cols + (c + 1) * ck])
        if (c + 1) * ck <= sb_w:
            p = p * sb_scale
        sb_ref[:, c * ck:(c + 1) * ck] = p.astype(BF16)


def _in_proj(x, norm_g, w_in, *, hg_cols, sb_w, sb_scale, tm=512, ck=512):
    n, d = x.shape
    return pl.pallas_call(
        functools.partial(_inproj_kernel, hg_cols=hg_cols, sb_w=sb_w, sb_scale=sb_scale, ck=ck),
        grid=(n // tm,),
        in_specs=[
            pl.BlockSpec((tm, d), lambda i: (i, 0)),
            _resident((1, d)),
            _resident(w_in.shape),
        ],
        out_specs=[
            pl.BlockSpec((tm, hg_cols), lambda i: (i, 0)),
            pl.BlockSpec((tm, 3 * sb_w), lambda i: (i, 0)),
        ],
        out_shape=[
            jax.ShapeDtypeStruct((n, hg_cols), F32),
            jax.ShapeDtypeStruct((n, 3 * sb_w), BF16),
        ],
        compiler_params=_params("parallel"),
        name="in_proj",
    )(x, norm_g.reshape(1, d), w_in)


def _hgrn_kernel(q_ref, f_ref, i_ref, g_ref, lbraw_ref, gn_ref, tri_ref, o_ref,
                 st_ref, b_ref, kk_ref, safe_ref, *, rows, hd, layer, nblk):
    t = pl.program_id(1)
    cur = lax.rem(t, 2)
    prev = 1 - cur
    n_chunks = rows // CHUNK
    n_heads = q_ref.shape[1] // hd
    n_sub = CHUNK // SUB

    @pl.when(t == 0)
    def _():
        st_ref[...] = jnp.zeros_like(st_ref)
        b_ref[...] = jnp.zeros_like(b_ref)
        kk_ref[...] = jnp.zeros_like(kk_ref)
        safe_ref[1] = 1

    gn = gn_ref[...]

    def prepare():
        raw = lbraw_ref[...]
        e = jnp.exp(raw - jnp.max(raw, axis=0, keepdims=True))
        lb = jnp.sum(e[0:layer + 1, :], axis=0, keepdims=True) / jnp.sum(e, axis=0, keepdims=True)
        tri = tri_ref[...]
        f = lb + (1.0 - lb) / (1.0 + jnp.exp(-f_ref[...]))
        kk_ref[cur] = 1.0 - f
        logf = jnp.log(f)
        hi = logf.astype(BF16)
        lo = (logf - hi.astype(F32)).astype(BF16)
        b_min = None
        for c in range(n_chunks):
            rs = slice(c * CHUNK, (c + 1) * CHUNK)
            b_c = _dot(tri, hi[rs]) + _dot(tri, lo[rs])
            b_ref[cur, rs, :] = b_c
            m = jnp.min(b_c[CHUNK - 1:CHUNK, :])
            b_min = m if b_min is None else jnp.minimum(b_min, m)
        safe_ref[cur] = (b_min >= -HG_SAFE_DECAY).astype(jnp.int32)

    def head_norm(o):
        return o * lax.rsqrt(jnp.mean(o * o, axis=-1, keepdims=True) + EPS)

    def silu(x):
        return x / (1.0 + jnp.exp(-x))

    @pl.when(safe_ref[prev] == 1)
    def _():
        assert n_chunks == 4
        b, kk = b_ref[prev], kk_ref[prev]
        q, v, gate = q_ref[...], i_ref[...], g_ref[...]
        last = [b[(c + 1) * CHUNK - 1:(c + 1) * CHUNK, :] for c in range(n_chunks)]

        def per_chunk(vals):
            return jnp.concatenate([jnp.broadcast_to(x, (CHUNK, x.shape[1])) for x in vals], axis=0)

        one = jnp.ones_like(last[0])
        el = [jnp.exp(x) for x in last]
        eb = jnp.exp(b)
        qa = q * eb
        ka = kk / eb
        kd = ka * per_chunk(el)
        q0 = qa * per_chunk([one, el[0], el[0] * el[1], el[0] * el[1] * el[2]])
        kend = kd * per_chunk([el[1] * el[2] * el[3], el[2] * el[3], el[3], one])
        q2 = qa[3 * CHUNK:] * el[2]
        kd2 = kd[:CHUNK] * el[1]
        e_tot = el[0] * el[1] * el[2] * el[3]
        qa, ka, kd, q0, kend, q2, kd2 = (x.astype(BF16) for x in (qa, ka, kd, q0, kend, q2, kd2))
        q2 = jnp.concatenate([qa[:3 * CHUNK], q2], axis=0)
        kd2 = jnp.concatenate([kd2, kd[CHUNK:]], axis=0)
        v16 = v.astype(BF16)

        ri = lax.broadcasted_iota(jnp.int32, (rows, rows), 0)
        ci = lax.broadcasted_iota(jnp.int32, (rows, rows), 1)
        rc, cc = ri // CHUNK, ci // CHUNK
        same_chunk = (rc == cc) & (ci <= ri)
        pair_chunk = (rc == cc + 1) & (rc % 2 == 1)
        half_block = (rc >= 2) & (cc < 2)
        outs = []
        for h in range(n_heads):
            sl = slice(h * hd, (h + 1) * hd)
            a = jnp.where(same_chunk, _dot_nt(qa[:, sl], ka[:, sl]),
                          jnp.where(pair_chunk, _dot_nt(qa[:, sl], kd[:, sl]),
                                    jnp.where(half_block, _dot_nt(q2[:, sl], kd2[:, sl]), 0.0)))
            st = st_ref[h]
            o = _dot(a.astype(BF16), v16[:, sl]) + _dot_nt(q0[:, sl], st.astype(BF16))
            st_ref[h] = st * e_tot[:, sl] + _dot(v[:, sl].T.astype(BF16), kend[:, sl])
            outs.append(head_norm(o))
        y = jnp.concatenate(outs, axis=-1) * gn
        o_ref[...] = (y * silu(gate)).astype(o_ref.dtype)
        prepare()

    @pl.when(safe_ref[prev] != 1)
    def _():
        lane = lax.broadcasted_iota(jnp.int32, (SUB, CHUNK), 1)
        row = lax.broadcasted_iota(jnp.int32, (SUB, CHUNK), 0)

        def chunk_body(c, carry):
            r0 = pl.multiple_of(c * CHUNK, CHUNK)
            b = b_ref[prev, pl.ds(r0, CHUNK), :]
            kk = kk_ref[prev, pl.ds(r0, CHUNK), :]
            q = q_ref[pl.ds(r0, CHUNK), :]
            v = i_ref[pl.ds(r0, CHUNK), :]
            gate = g_ref[pl.ds(r0, CHUNK), :]
            b_last = b[CHUNK - 1:CHUNK, :]
            q_in = (q * jnp.exp(b)).astype(BF16)
            k_dec = (kk * jnp.exp(b_last - b)).astype(BF16)
            e_last = jnp.exp(b_last)
            v16 = v.astype(BF16)
            outs = []
            for h in range(n_heads):
                sl = slice(h * hd, (h + 1) * hd)
                qh, kh, bh, vh = q[:, sl], kk[:, sl], b[:, sl], v16[:, sl]
                vt = v[:, sl].T.astype(BF16)
                st = st_ref[h]
                o_inter = _dot_nt(q_in[:, sl], st.astype(BF16))
                a_rows = []
                for i in range(n_sub):
                    rs = slice(i * SUB, (i + 1) * SUB)
                    qi, ki, bi = qh[rs], kh[rs], bh[rs]
                    dwide = jnp.zeros((SUB, CHUNK), F32)
                    for s in range(SUB):
                        p = qi * (ki[s:s + 1] * jnp.exp(jnp.minimum(bi - bi[s:s + 1], 0.0)))
                        col = jnp.sum(p, axis=-1, keepdims=True)
                        dwide = jnp.where(lane == i * SUB + s, col, dwide)
                    a_i = jnp.where(lane <= i * SUB + row, dwide, 0.0)
                    if i > 0:
                        r_i = bh[i * SUB - 1:i * SUB]
                        qt = (qi * jnp.exp(bi - r_i)).astype(BF16)
                        kt = (kh * jnp.exp(jnp.minimum(r_i - bh, 0.0))).astype(BF16)
                        a_i = jnp.where(lane < i * SUB, _dot_nt(qt, kt), a_i)
                    a_rows.append(a_i)
                a = jnp.concatenate(a_rows, axis=0).astype(BF16)
                o = _dot(a, vh) + o_inter
                st_ref[h] = st * e_last[:, sl] + _dot(vt, k_dec[:, sl])
                outs.append(head_norm(o))
            y = jnp.concatenate(outs, axis=-1) * gn
            o_ref[pl.ds(r0, CHUNK), :] = (y * silu(gate)).astype(o_ref.dtype)
            return carry

        lax.fori_loop(0, n_chunks, chunk_body, 0)
        prepare()


def _hgrn(hg, lb_raw, gnorm, *, batch, seq, width, layer, rows=256):
    n = batch * seq
    nblk = seq // rows
    hd = width // HG_HEADS
    tri = jnp.tril(jnp.ones((CHUNK, CHUNK), BF16))

    def col_spec(j, lag):
        return pl.BlockSpec((rows, width),
                            lambda b, t: (b * nblk + jnp.clip(t - lag, 0, nblk - 1), j))

    return pl.pallas_call(
        functools.partial(_hgrn_kernel, rows=rows, hd=hd, layer=layer, nblk=nblk),
        grid=(batch, nblk + 1),
        in_specs=[col_spec(0, 1), col_spec(1, 0), col_spec(2, 1), col_spec(3, 1),
                  _resident(lb_raw.shape), _resident((1, width)), _resident((CHUNK, CHUNK))],
        out_specs=col_spec(0, 1),
        out_shape=jax.ShapeDtypeStruct((n, width), BF16),
        scratch_shapes=[pltpu.VMEM((HG_HEADS, hd, hd), F32),
                        pltpu.VMEM((2, rows, width), F32),
                        pltpu.VMEM((2, rows, width), F32),
                        pltpu.SMEM((2,), jnp.int32)],
        compiler_params=_params("parallel", "arbitrary"),
        name="hgrn",
    )(hg, hg, hg, hg, lb_raw, gnorm.reshape(1, width), tri)


def _sb_kernel(qa_ref, qc_ref, k_ref, v_ref, gn_ref, uu_ref, o_ref,
               z_ref, w_ref, rem_ref, acc_ref, live_ref, *, tq, hd, rs, win, nq):
    n_sub = tq // rs
    rows = 2 * rs
    t = pl.program_id(2)
    slot_a = lax.rem(t, 2)
    slot_b = 1 - slot_a
    uu = uu_ref[...]
    lane = lax.broadcasted_iota(jnp.int32, (rs, 2 * hd), 1)
    col = lax.broadcasted_iota(jnp.int32, (rows, win), 1)
    col_minus_row = col - lax.rem(lax.broadcasted_iota(jnp.int32, (rows, win), 0), rs)

    @pl.when(t == 0)
    def _():
        z_ref[...] = jnp.zeros_like(z_ref)
        w_ref[...] = jnp.zeros_like(w_ref)
        rem_ref[...] = jnp.zeros_like(rem_ref)
        acc_ref[...] = jnp.zeros_like(acc_ref)
        live_ref[0] = 0
        live_ref[1] = 0

    def frontier(t0, r, it):
        return pl.multiple_of(jnp.maximum(t0 + (r + 1) * rs - (it + 1) * win, 0), rs)

    def scores(q_ref, r, start):
        qr = q_ref[r * rs:(r + 1) * rs, :]
        zero = jnp.zeros_like(qr)
        qq = jnp.concatenate([jnp.where(lane < hd, qr, zero), jnp.where(lane < hd, zero, qr)], axis=0)
        return _dot_nt(qq, k_ref[pl.ds(start, win), :])

    def weights(zs, valids, rem):
        z = jnp.concatenate([jnp.where(valids[r], zs[r], -1e30) for r in range(n_sub)], axis=0)
        sp = jnp.maximum(z, 0.0) + jnp.log(1.0 + jnp.exp2(jnp.abs(z) * (-LOG2E)))
        hi = sp.astype(BF16)
        lo = (sp - hi.astype(F32)).astype(BF16)
        c = _dot(jnp.concatenate([hi, lo], axis=1), uu)
        if rem is not None:
            c = c + rem
        return jnp.exp2((z + c) * LOG2E).astype(BF16), c[:, 0:1]

    def output(w, r, start):
        pv = _dot(w, v_ref[pl.ds(start, win), :])
        return jnp.where(lane < hd, pv[:rs], pv[rs:])

    lane_q = lax.broadcasted_iota(jnp.int32, (tq, 2 * hd), 1)
    o = acc_ref[slot_b].reshape(tq, 2 * hd)
    o2 = o * o
    ms0 = jnp.sum(jnp.where(lane_q < hd, o2, 0.0), axis=-1, keepdims=True)
    ms1 = jnp.sum(jnp.where(lane_q < hd, 0.0, o2), axis=-1, keepdims=True)
    ms = jnp.where(lane_q < hd, ms0, ms1) * (1.0 / hd)
    o_ref[...] = (o * lax.rsqrt(ms + EPS) * gn_ref[...]).astype(o_ref.dtype)

    t0c = jnp.clip(t - 2, 0, nq - 1) * tq
    for r in range(n_sub):
        acc_ref[slot_a, r] = output(w_ref[slot_a, r * rows:(r + 1) * rows, :], r, frontier(t0c, r, 0))

    t0b = jnp.clip(t - 1, 0, nq - 1) * tq
    zs = [z_ref[slot_b, r * rows:(r + 1) * rows, :] for r in range(n_sub)]
    valids = [col_minus_row < t0b + r * rs - frontier(t0b, r, 0) for r in range(n_sub)]
    w, rem = weights(zs, valids, None)
    w_ref[slot_b] = w
    rem_ref[slot_b] = rem
    live_ref[slot_b] = (jnp.max(rem) >= SB_LOG_ZERO).astype(jnp.int32)

    t0a = jnp.minimum(t, nq - 1) * tq
    for r in range(n_sub):
        z_ref[slot_a, r * rows:(r + 1) * rows, :] = scores(qa_ref, r, frontier(t0a, r, 0))

    def cond(carry):
        it, live = carry
        return (frontier(t0c, n_sub - 1, it - 1) > 0) & (live > 0)

    def body(carry):
        it, _ = carry
        starts = [frontier(t0c, r, it) for r in range(n_sub)]
        zs = [scores(qc_ref, r, starts[r]) for r in range(n_sub)]
        valids = [col < frontier(t0c, r, it - 1) - starts[r] for r in range(n_sub)]
        w, rem = weights(zs, valids, rem_ref[slot_a])
        for r in range(n_sub):
            acc_ref[slot_a, r] += output(w[r * rows:(r + 1) * rows], r, starts[r])
        rem_ref[slot_a] = rem
        return it + 1, (jnp.max(rem) >= SB_LOG_ZERO).astype(jnp.int32)

    lax.while_loop(cond, body, (jnp.int32(1), live_ref[slot_a]))


def _sb(sbp, gnorm, *, batch, seq, width, tq=256, rs=64, win=256):
    n = batch * seq
    hd = width // SB_HEADS
    pair = 2 * hd
    n_pair = width // pair
    nq = seq // tq
    n_sub = tq // rs
    u = -(jnp.arange(win)[:, None] >= jnp.arange(win)[None, :]).astype(BF16)

    def q_spec(lag):
        return pl.BlockSpec((tq, pair), lambda b, p, t: (b * nq + jnp.clip(t - lag, 0, nq - 1), p))

    return pl.pallas_call(
        functools.partial(_sb_kernel, tq=tq, hd=hd, rs=rs, win=win, nq=nq),
        grid=(batch, n_pair, nq + 3),
        in_specs=[
            q_spec(0),
            q_spec(2),
            pl.BlockSpec((seq, pair), lambda b, p, t: (b, n_pair + p)),
            pl.BlockSpec((seq, pair), lambda b, p, t: (b, 2 * n_pair + p)),
            pl.BlockSpec((1, pair), lambda b, p, t: (0, p)),
            _resident((2 * win, win)),
        ],
        out_specs=q_spec(3),
        out_shape=jax.ShapeDtypeStruct((n, width), BF16),
        scratch_shapes=[
            pltpu.VMEM((2, n_sub * 2 * rs, win), F32),
            pltpu.VMEM((2, n_sub * 2 * rs, win), BF16),
            pltpu.VMEM((2, n_sub * 2 * rs, 1), F32),
            pltpu.VMEM((2, n_sub, rs, pair), F32),
            pltpu.SMEM((2,), jnp.int32),
        ],
        compiler_params=_params("parallel", "parallel", "arbitrary"),
        name="sb",
    )(sbp, sbp, sbp, sbp, gnorm.reshape(1, width), jnp.concatenate([u, u], axis=0))


def _memkv_kernel(m_ref, g_ref, w_ref, o_ref):
    h = _rmsnorm(m_ref[...], g_ref[...]).astype(BF16)
    o_ref[...] = _dot(h, w_ref[...]).astype(o_ref.dtype)


def _mem_kv(mem2d, norm_g, w_kv, *, tm=256):
    n, d = mem2d.shape
    return pl.pallas_call(
        _memkv_kernel,
        grid=(n // tm,),
        in_specs=[pl.BlockSpec((tm, d), lambda i: (i, 0)), _resident((1, d)), _resident(w_kv.shape)],
        out_specs=pl.BlockSpec((tm, w_kv.shape[1]), lambda i: (i, 0)),
        out_shape=jax.ShapeDtypeStruct((n, w_kv.shape[1]), BF16),
        compiler_params=_params("parallel"),
        name="mem_kv",
    )(mem2d, norm_g.reshape(1, d), w_kv)


def _post_kernel(x_ref, ohg_ref, osb_ref, wout_ref, qg_ref, wq_ref, kv_ref, wo_ref, o_ref, att_ref,
                 *, hg_w, heads):
    d = x_ref.shape[1]
    dh = d // heads
    x = x_ref[...] + _dot(ohg_ref[...], wout_ref[:hg_w, :]) + _dot(osb_ref[...], wout_ref[hg_w:, :])
    hq = _rmsnorm(x, qg_ref[...]).astype(BF16)
    q = (_dot(hq, wq_ref[...]) * (dh ** -0.5)).astype(BF16)
    for h in range(heads):
        k = kv_ref[:, h * dh:(h + 1) * dh]
        v = kv_ref[:, d + h * dh:d + (h + 1) * dh]
        s = _dot_nt(q[:, h * dh:(h + 1) * dh], k)
        e = jnp.exp(s - jnp.max(s, axis=-1, keepdims=True))
        p = e / jnp.sum(e, axis=-1, keepdims=True)
        att_ref[:, h * dh:(h + 1) * dh] = _dot(p.astype(BF16), v).astype(BF16)
    o_ref[...] = x + _dot(att_ref[...], wo_ref[...])


def _post(x, o_hg, o_sb, w_out, q_norm, w_q, kv, w_o, *, seq, n_mem, tm=512):
    n, d = x.shape
    hg_w = o_hg.shape[1]
    per_batch = seq // tm
    return pl.pallas_call(
        functools.partial(_post_kernel, hg_w=hg_w, heads=MEM_HEADS),
        grid=(n // tm,),
        in_specs=[
            pl.BlockSpec((tm, d), lambda i: (i, 0)),
            pl.BlockSpec((tm, hg_w), lambda i: (i, 0)),
            pl.BlockSpec((tm, o_sb.shape[1]), lambda i: (i, 0)),
            _resident(w_out.shape),
            _resident((1, d)),
            _resident(w_q.shape),
            pl.BlockSpec((n_mem, 2 * d), lambda i: (i // per_batch, 0)),
            _resident(w_o.shape),
        ],
        out_specs=pl.BlockSpec((tm, d), lambda i: (i, 0)),
        out_shape=jax.ShapeDtypeStruct((n, d), F32),
        scratch_shapes=[pltpu.VMEM((tm, d), BF16)],
        compiler_params=_params("parallel"),
        name="post",
    )(x, o_hg, o_sb, w_out, q_norm.reshape(1, d), w_q, kv, w_o)


def kernel(x, mem, ffn1_norm, ffn1_w_gu, ffn1_w_down, mix_norm, w_in, hg_lb_raw, hg_gnorm, sb_gnorm,
           w_out, mem_q_norm, mem_kv_norm, mem_w_q, mem_w_kv, mem_w_o, ffn2_norm, ffn2_w_gu,
           ffn2_w_down, final_norm):
    batch, seq, d = x.shape
    n_mem = mem.shape[1]
    depth = ffn1_norm.shape[0]
    hg_w = hg_gnorm.shape[1]
    sb_w = sb_gnorm.shape[1]
    sb_scale = (sb_w // SB_HEADS) ** -0.5

    xs = x.reshape(batch * seq, d)
    mem2d = mem.reshape(batch * n_mem, d)
    for l in range(depth):
        xs = _ffn(xs, ffn1_norm[l], ffn1_w_gu[l].astype(BF16), ffn1_w_down[l].astype(BF16))
        hg, sbp = _in_proj(xs, mix_norm[l], w_in[l].astype(BF16),
                           hg_cols=4 * hg_w, sb_w=sb_w, sb_scale=sb_scale)
        o_hg = _hgrn(hg, hg_lb_raw, hg_gnorm[l], batch=batch, seq=seq, width=hg_w, layer=l)
        o_sb = _sb(sbp, sb_gnorm[l], batch=batch, seq=seq, width=sb_w)
        kv = _mem_kv(mem2d, mem_kv_norm[l], mem_w_kv[l].astype(BF16))
        xs = _post(xs, o_hg, o_sb, w_out[l].astype(BF16), mem_q_norm[l], mem_w_q[l].astype(BF16),
                   kv, mem_w_o[l].astype(BF16), seq=seq, n_mem=n_mem)
        last = l == depth - 1
        xs = _ffn(xs, ffn2_norm[l], ffn2_w_gu[l].astype(BF16), ffn2_w_down[l].astype(BF16),
                  final_norm if last else None)
    return xs.reshape(batch, seq, d)
```

```python
import functools

import jax
import jax.numpy as jnp
from jax import lax
from jax.experimental import pallas as pl
from jax.experimental.pallas import tpu as pltpu

F32 = jnp.float32
BF16 = jnp.bfloat16

EPS = 1e-6
CHUNK = 64
SUB = 16
HG_HEADS = 4
SB_HEADS = 8
MEM_HEADS = 4
HG_SAFE_DECAY = 80.0
SB_LOG_ZERO = -104.0
LOG2E = 1.4426950408889634

V7X_VMEM_LIMIT = 56 * 1024 * 1024


def _params(*sem):
    return pltpu.CompilerParams(dimension_semantics=sem, vmem_limit_bytes=V7X_VMEM_LIMIT)


def _resident(shape):
    nd = len(shape)
    return pl.BlockSpec(shape, lambda *_: (0,) * nd, pipeline_mode=pl.Buffered(1))


def _rmsnorm(x, g):
    return x * lax.rsqrt(jnp.mean(x * x, axis=-1, keepdims=True) + EPS) * g


def _dot(a, b):
    return jnp.dot(a, b, preferred_element_type=F32)


def _dot_nt(a, b):
    return lax.dot_general(a, b, (((1,), (1,)), ((), ())), preferred_element_type=F32)


def _split3(x):
    hi = x.astype(BF16)
    r = x - hi.astype(F32)
    mid = r.astype(BF16)
    lo = (r - mid.astype(F32)).astype(BF16)
    return hi, mid, lo


def _ffn_kernel(x_ref, g_ref, wgu_ref, wd_ref, *rest, d_ff, ck, final):
    if final:
        fg_ref, o_ref, act_ref = rest
    else:
        o_ref, act_ref = rest
    x = x_ref[...]
    h = _rmsnorm(x, g_ref[...]).astype(BF16)
    for c in range(d_ff // ck):
        gate = _dot(h, wgu_ref[:, c * ck:(c + 1) * ck])
        up = _dot(h, wgu_ref[:, d_ff + c * ck:d_ff + (c + 1) * ck])
        silu = gate / (1.0 + jnp.exp(-gate))
        act_ref[:, c * ck:(c + 1) * ck] = (silu * up).astype(BF16)
    y = _dot(act_ref[...], wd_ref[...])
    out = x + 0.5 * y
    if final:
        out = _rmsnorm(out, fg_ref[...])
    o_ref[...] = out


def _ffn(x, norm_g, w_gu, w_down, final_g=None, *, tm=512, ck=256):
    n, d = x.shape
    d_ff = w_down.shape[0]
    final = final_g is not None
    in_specs = [
        pl.BlockSpec((tm, d), lambda i: (i, 0)),
        _resident((1, d)),
        _resident((d, 2 * d_ff)),
        _resident((d_ff, d)),
    ]
    args = [x, norm_g.reshape(1, d), w_gu, w_down]
    if final:
        in_specs.append(_resident((1, d)))
        args.append(final_g.reshape(1, d))
    return pl.pallas_call(
        functools.partial(_ffn_kernel, d_ff=d_ff, ck=ck, final=final),
        grid=(n // tm,),
        in_specs=in_specs,
        out_specs=pl.BlockSpec((tm, d), lambda i: (i, 0)),
        out_shape=jax.ShapeDtypeStruct((n, d), F32),
        scratch_shapes=[pltpu.VMEM((tm, d_ff), BF16)],
        compiler_params=_params("parallel"),
        name="ffn_final" if final else "ffn",
    )(*args)


def _inproj_kernel(x_ref, g_ref, w_ref, hg_ref, sb_ref, *, hg_cols, sb_w, sb_scale, ck):
    h = _rmsnorm(x_ref[...], g_ref[...]).astype(BF16)
    for c in range(hg_cols // ck):
        hg_ref[:, c * ck:(c + 1) * ck] = _dot(h, w_ref[:, c * ck:(c + 1) * ck])
    for c in range(3 * sb_w // ck):
        p = _dot(h, w_ref[:, hg_cols + c * ck:hg_cols + (c + 1) * ck])
        if (c + 1) * ck <= sb_w:
            p = p * sb_scale
        sb_ref[:, c * ck:(c + 1) * ck] = p.astype(BF16)


def _in_proj(x, norm_g, w_in, *, hg_cols, sb_w, sb_scale, tm=512, ck=512):
    n, d = x.shape
    return pl.pallas_call(
        functools.partial(_inproj_kernel, hg_cols=hg_cols, sb_w=sb_w, sb_scale=sb_scale, ck=ck),
        grid=(n // tm,),
        in_specs=[
            pl.BlockSpec((tm, d), lambda i: (i, 0)),
            _resident((1, d)),
            _resident(w_in.shape),
        ],
        out_specs=[
            pl.BlockSpec((tm, hg_cols), lambda i: (i, 0)),
            pl.BlockSpec((tm, 3 * sb_w), lambda i: (i, 0)),
        ],
        out_shape=[
            jax.ShapeDtypeStruct((n, hg_cols), F32),
            jax.ShapeDtypeStruct((n, 3 * sb_w), BF16),
        ],
        compiler_params=_params("parallel"),
        name="in_proj",
    )(x, norm_g.reshape(1, d), w_in)


def _hgrn_kernel(q_ref, f_ref, i_ref, g_ref, lbraw_ref, gn_ref, tri_ref, o_ref,
                 st_ref, b_ref, kk_ref, safe_ref, *, rows, hd, layer, nblk):
    t = pl.program_id(1)
    cur = lax.rem(t, 2)
    prev = 1 - cur
    n_chunks = rows // CHUNK
    n_heads = q_ref.shape[1] // hd
    n_sub = CHUNK // SUB

    @pl.when(t == 0)
    def _():
        st_ref[...] = jnp.zeros_like(st_ref)
        b_ref[...] = jnp.zeros_like(b_ref)
        kk_ref[...] = jnp.zeros_like(kk_ref)
        safe_ref[1] = 1

    gn = gn_ref[...]

    def prepare():
        raw = lbraw_ref[...]
        e = jnp.exp(raw - jnp.max(raw, axis=0, keepdims=True))
        lb = jnp.sum(e[0:layer + 1, :], axis=0, keepdims=True) / jnp.sum(e, axis=0, keepdims=True)
        tri = tri_ref[...]
        f = lb + (1.0 - lb) / (1.0 + jnp.exp(-f_ref[...]))
        kk_ref[cur] = 1.0 - f
        logf = jnp.log(f)
        hi = logf.astype(BF16)
        lo = (logf - hi.astype(F32)).astype(BF16)
        b_min = None
        for c in range(n_chunks):
            rs = slice(c * CHUNK, (c + 1) * CHUNK)
            b_c = _dot(tri, hi[rs]) + _dot(tri, lo[rs])
            b_ref[cur, rs, :] = b_c
            m = jnp.min(b_c[CHUNK - 1:CHUNK, :])
            b_min = m if b_min is None else jnp.minimum(b_min, m)
        safe_ref[cur] = (b_min >= -HG_SAFE_DECAY).astype(jnp.int32)

    def head_norm(o):
        return o * lax.rsqrt(jnp.mean(o * o, axis=-1, keepdims=True) + EPS)

    def silu(x):
        return x / (1.0 + jnp.exp(-x))

    @pl.when(safe_ref[prev] == 1)
    def _():
        assert n_chunks == 4
        b, kk = b_ref[prev], kk_ref[prev]
        q, v, gate = q_ref[...], i_ref[...], g_ref[...]
        last = [b[(c + 1) * CHUNK - 1:(c + 1) * CHUNK, :] for c in range(n_chunks)]

        def per_chunk(vals):
            return jnp.concatenate([jnp.broadcast_to(x, (CHUNK, x.shape[1])) for x in vals], axis=0)

        one = jnp.ones_like(last[0])
        el = [jnp.exp(x) for x in last]
        eb = jnp.exp(b)
        qa = q * eb
        ka = kk / eb
        kd = ka * per_chunk(el)
        q0 = qa * per_chunk([one, el[0], el[0] * el[1], el[0] * el[1] * el[2]])
        kend = kd * per_chunk([el[1] * el[2] * el[3], el[2] * el[3], el[3], one])
        q2 = qa[3 * CHUNK:] * el[2]
        kd2 = kd[:CHUNK] * el[1]
        e_tot = el[0] * el[1] * el[2] * el[3]
        qa, ka, kd, q0, kend, q2, kd2 = (x.astype(BF16) for x in (qa, ka, kd, q0, kend, q2, kd2))
        q2 = jnp.concatenate([qa[:3 * CHUNK], q2], axis=0)
        kd2 = jnp.concatenate([kd2, kd[CHUNK:]], axis=0)
        v16 = v.astype(BF16)

        ri = lax.broadcasted_iota(jnp.int32, (rows, rows), 0)
        ci = lax.broadcasted_iota(jnp.int32, (rows, rows), 1)
        rc, cc = ri // CHUNK, ci // CHUNK
        same_chunk = (rc == cc) & (ci <= ri)
        pair_chunk = (rc == cc + 1) & (rc % 2 == 1)
        half_block = (rc >= 2) & (cc < 2)
        outs = []
        for h in range(n_heads):
            sl = slice(h * hd, (h + 1) * hd)
            a = jnp.where(same_chunk, _dot_nt(qa[:, sl], ka[:, sl]),
                          jnp.where(pair_chunk, _dot_nt(qa[:, sl], kd[:, sl]),
                                    jnp.where(half_block, _dot_nt(q2[:, sl], kd2[:, sl]), 0.0)))
            st = st_ref[h]
            o = _dot(a.astype(BF16), v16[:, sl]) + _dot_nt(q0[:, sl], st.astype(BF16))
            st_ref[h] = st * e_tot[:, sl] + _dot(v[:, sl].T.astype(BF16), kend[:, sl])
            outs.append(head_norm(o))
        y = jnp.concatenate(outs, axis=-1) * gn
        o_ref[...] = (y * silu(gate)).astype(o_ref.dtype)
        prepare()

    @pl.when(safe_ref[prev] != 1)
    def _():
        lane = lax.broadcasted_iota(jnp.int32, (SUB, CHUNK), 1)
        row = lax.broadcasted_iota(jnp.int32, (SUB, CHUNK), 0)

        def chunk_body(c, carry):
            r0 = pl.multiple_of(c * CHUNK, CHUNK)
            b = b_ref[prev, pl.ds(r0, CHUNK), :]
            kk = kk_ref[prev, pl.ds(r0, CHUNK), :]
            q = q_ref[pl.ds(r0, CHUNK), :]
            v = i_ref[pl.ds(r0, CHUNK), :]
            gate = g_ref[pl.ds(r0, CHUNK), :]
            b_last = b[CHUNK - 1:CHUNK, :]
            q_in = (q * jnp.exp(b)).astype(BF16)
            k_dec = (kk * jnp.exp(b_last - b)).astype(BF16)
            e_last = jnp.exp(b_last)
            v16 = v.astype(BF16)
            outs = []
            for h in range(n_heads):
                sl = slice(h * hd, (h + 1) * hd)
                qh, kh, bh, vh = q[:, sl], kk[:, sl], b[:, sl], v16[:, sl]
                vt = v[:, sl].T.astype(BF16)
                st = st_ref[h]
                o_inter = _dot_nt(q_in[:, sl], st.astype(BF16))
                a_rows = []
                for i in range(n_sub):
                    rs = slice(i * SUB, (i + 1) * SUB)
                    qi, ki, bi = qh[rs], kh[rs], bh[rs]
                    dwide = jnp.zeros((SUB, CHUNK), F32)
                    for s in range(SUB):
                        p = qi * (ki[s:s + 1] * jnp.exp(jnp.minimum(bi - bi[s:s + 1], 0.0)))
                        col = jnp.sum(p, axis=-1, keepdims=True)
                        dwide = jnp.where(lane == i * SUB + s, col, dwide)
                    a_i = jnp.where(lane <= i * SUB + row, dwide, 0.0)
                    if i > 0:
                        r_i = bh[i * SUB - 1:i * SUB]
                        qt = (qi * jnp.exp(bi - r_i)).astype(BF16)
                        kt = (kh * jnp.exp(jnp.minimum(r_i - bh, 0.0))).astype(BF16)
                        a_i = jnp.where(lane < i * SUB, _dot_nt(qt, kt), a_i)
                    a_rows.append(a_i)
                a = jnp.concatenate(a_rows, axis=0).astype(BF16)
                o = _dot(a, vh) + o_inter
                st_ref[h] = st * e_last[:, sl] + _dot(vt, k_dec[:, sl])
                outs.append(head_norm(o))
            y = jnp.concatenate(outs, axis=-1) * gn
            o_ref[pl.ds(r0, CHUNK), :] = (y * silu(gate)).astype(o_ref.dtype)
            return carry

        lax.fori_loop(0, n_chunks, chunk_body, 0)
        prepare()


def _hgrn(hg, lb_raw, gnorm, *, batch, seq, width, layer, rows=256):
    n = batch * seq
    nblk = seq // rows
    hd = width // HG_HEADS
    tri = jnp.tril(jnp.ones((CHUNK, CHUNK), BF16))

    def col_spec(j, lag):
        return pl.BlockSpec((rows, width),
                            lambda b, t: (b * nblk + jnp.clip(t - lag, 0, nblk - 1), j))

    return pl.pallas_call(
        functools.partial(_hgrn_kernel, rows=rows, hd=hd, layer=layer, nblk=nblk),
        grid=(batch, nblk + 1),
        in_specs=[col_spec(0, 1), col_spec(1, 0), col_spec(2, 1), col_spec(3, 1),
                  _resident(lb_raw.shape), _resident((1, width)), _resident((CHUNK, CHUNK))],
        out_specs=col_spec(0, 1),
        out_shape=jax.ShapeDtypeStruct((n, width), BF16),
        scratch_shapes=[pltpu.VMEM((HG_HEADS, hd, hd), F32),
                        pltpu.VMEM((2, rows, width), F32),
                        pltpu.VMEM((2, rows, width), F32),
                        pltpu.SMEM((2,), jnp.int32)],
        compiler_params=_params("parallel", "arbitrary"),
        name="hgrn",
    )(hg, hg, hg, hg, lb_raw, gnorm.reshape(1, width), tri)


def _sb_kernel(q_ref, k_ref, v_ref, gn_ref, uu_ref, o_ref,
               z_ref, w_ref, rem_ref, acc_ref, live_ref, *, tq, hd, rs, win, nq):
    n_sub = tq // rs
    rows = 2 * rs
    uu = uu_ref[...]
    gn = gn_ref[...]
    lane = lax.broadcasted_iota(jnp.int32, (rs, 2 * hd), 1)
    lane_q = lax.broadcasted_iota(jnp.int32, (tq, 2 * hd), 1)
    col = lax.broadcasted_iota(jnp.int32, (rs, win), 1)
    col_minus_row = col - lax.broadcasted_iota(jnp.int32, (rs, win), 0)

    z_ref[...] = jnp.zeros_like(z_ref)
    w_ref[...] = jnp.zeros_like(w_ref)
    rem_ref[...] = jnp.zeros_like(rem_ref)
    acc_ref[...] = jnp.zeros_like(acc_ref)
    live_ref[0] = 0
    live_ref[1] = 0

    def frontier(t0, r, it):
        return pl.multiple_of(jnp.maximum(t0 + (r + 1) * rs - (it + 1) * win, 0), rs)

    def scores(t0, r, start):
        qr = q_ref[pl.ds(pl.multiple_of(t0 + r * rs, rs), rs), :]
        zero = jnp.zeros_like(qr)
        qq = jnp.concatenate([jnp.where(lane < hd, qr, zero), jnp.where(lane < hd, zero, qr)], axis=0)
        return _dot_nt(qq, k_ref[pl.ds(start, win), :])

    def weights(zs, valids, rem):
        z = jnp.concatenate(
            [jnp.where(jnp.concatenate([valids[r], valids[r]], axis=0), zs[r], -1e30) for r in range(n_sub)],
            axis=0)
        sp = jnp.maximum(z, 0.0) + jnp.log(1.0 + jnp.exp2(jnp.abs(z) * (-LOG2E)))
        hi = sp.astype(BF16)
        lo = (sp - hi.astype(F32)).astype(BF16)
        c = _dot(jnp.concatenate([hi, lo], axis=1), uu)
        if rem is not None:
            c = c + rem
        return jnp.exp2((z + c) * LOG2E).astype(BF16), c[:, 0:1]

    def output(w, start):
        pv = _dot(w, v_ref[pl.ds(start, win), :])
        return jnp.where(lane < hd, pv[:rs], pv[rs:])

    def step(t, carry):
        slot_a = lax.rem(t, 2)
        slot_b = 1 - slot_a
        t0d = pl.multiple_of(jnp.clip(t - 3, 0, nq - 1) * tq, tq)
        o = acc_ref[slot_b].reshape(tq, 2 * hd)
        o2 = o * o
        ms0 = jnp.sum(jnp.where(lane_q < hd, o2, 0.0), axis=-1, keepdims=True)
        ms1 = jnp.sum(jnp.where(lane_q < hd, 0.0, o2), axis=-1, keepdims=True)
        ms = jnp.where(lane_q < hd, ms0, ms1) * (1.0 / hd)
        o_ref[pl.ds(t0d, tq), :] = (o * lax.rsqrt(ms + EPS) * gn).astype(o_ref.dtype)

        t0c = jnp.clip(t - 2, 0, nq - 1) * tq
        for r in range(n_sub):
            acc_ref[slot_a, r] = output(w_ref[slot_a, r * rows:(r + 1) * rows, :], frontier(t0c, r, 0))

        t0b = jnp.clip(t - 1, 0, nq - 1) * tq
        zs = [z_ref[slot_b, r * rows:(r + 1) * rows, :] for r in range(n_sub)]
        valids = [col_minus_row < t0b + r * rs - frontier(t0b, r, 0) for r in range(n_sub)]
        w, rem = weights(zs, valids, None)
        w_ref[slot_b] = w
        rem_ref[slot_b] = rem
        live_ref[slot_b] = (jnp.max(rem) >= SB_LOG_ZERO).astype(jnp.int32)

        t0a = jnp.minimum(t, nq - 1) * tq
        for r in range(n_sub):
            z_ref[slot_a, r * rows:(r + 1) * rows, :] = scores(t0a, r, frontier(t0a, r, 0))

        def cond(c):
            it, live = c
            return (frontier(t0c, n_sub - 1, it - 1) > 0) & (live > 0)

        def body(c):
            it, _ = c
            starts = [frontier(t0c, r, it) for r in range(n_sub)]
            zs = [scores(t0c, r, starts[r]) for r in range(n_sub)]
            valids = [col < frontier(t0c, r, it - 1) - starts[r] for r in range(n_sub)]
            w, rem = weights(zs, valids, rem_ref[slot_a])
            for r in range(n_sub):
                acc_ref[slot_a, r] += output(w[r * rows:(r + 1) * rows], starts[r])
            rem_ref[slot_a] = rem
            return it + 1, (jnp.max(rem) >= SB_LOG_ZERO).astype(jnp.int32)

        lax.while_loop(cond, body, (jnp.int32(1), live_ref[slot_a]))
        return carry

    lax.fori_loop(0, nq + 3, step, 0)


def _sb(sbp, gnorm, *, batch, seq, width, tq=256, rs=64, win=256):
    n = batch * seq
    hd = width // SB_HEADS
    pair = 2 * hd
    n_pair = width // pair
    nq = seq // tq
    n_sub = tq // rs
    u = -(jnp.arange(win)[:, None] >= jnp.arange(win)[None, :]).astype(BF16)

    def seq_spec(group):
        return pl.BlockSpec((seq, pair), lambda b, p: (b, group * n_pair + p))

    return pl.pallas_call(
        functools.partial(_sb_kernel, tq=tq, hd=hd, rs=rs, win=win, nq=nq),
        grid=(batch, n_pair),
        in_specs=[
            seq_spec(0), seq_spec(1), seq_spec(2),
            pl.BlockSpec((1, pair), lambda b, p: (0, p)),
            _resident((2 * win, win)),
        ],
        out_specs=seq_spec(0),
        out_shape=jax.ShapeDtypeStruct((n, width), BF16),
        scratch_shapes=[
            pltpu.VMEM((2, n_sub * 2 * rs, win), F32),
            pltpu.VMEM((2, n_sub * 2 * rs, win), BF16),
            pltpu.VMEM((2, n_sub * 2 * rs, 1), F32),
            pltpu.VMEM((2, n_sub, rs, pair), F32),
            pltpu.SMEM((2,), jnp.int32),
        ],
        compiler_params=_params("parallel", "parallel"),
        name="sb",
    )(sbp, sbp, sbp, gnorm.reshape(1, width), jnp.concatenate([u, u], axis=0))


def _memkv_kernel(m_ref, g_ref, w_ref, o_ref):
    h = _rmsnorm(m_ref[...], g_ref[...]).astype(BF16)
    o_ref[...] = _dot(h, w_ref[...]).astype(o_ref.dtype)


def _mem_kv(mem2d, norm_g, w_kv, *, tm=256):
    n, d = mem2d.shape
    return pl.pallas_call(
        _memkv_kernel,
        grid=(n // tm,),
        in_specs=[pl.BlockSpec((tm, d), lambda i: (i, 0)), _resident((1, d)), _resident(w_kv.shape)],
        out_specs=pl.BlockSpec((tm, w_kv.shape[1]), lambda i: (i, 0)),
        out_shape=jax.ShapeDtypeStruct((n, w_kv.shape[1]), BF16),
        compiler_params=_params("parallel"),
        name="mem_kv",
    )(mem2d, norm_g.reshape(1, d), w_kv)


def _post_kernel(x_ref, ohg_ref, osb_ref, wout_ref, qg_ref, wq_ref, kv_ref, wo_ref, o_ref, att_ref,
                 *, hg_w, heads):
    d = x_ref.shape[1]
    dh = d // heads
    x = x_ref[...] + _dot(ohg_ref[...], wout_ref[:hg_w, :]) + _dot(osb_ref[...], wout_ref[hg_w:, :])
    hq = _rmsnorm(x, qg_ref[...]).astype(BF16)
    q = (_dot(hq, wq_ref[...]) * (dh ** -0.5)).astype(BF16)
    for h in range(heads):
        k = kv_ref[:, h * dh:(h + 1) * dh]
        v = kv_ref[:, d + h * dh:d + (h + 1) * dh]
        s = _dot_nt(q[:, h * dh:(h + 1) * dh], k)
        e = jnp.exp(s - jnp.max(s, axis=-1, keepdims=True))
        p = e / jnp.sum(e, axis=-1, keepdims=True)
        att_ref[:, h * dh:(h + 1) * dh] = _dot(p.astype(BF16), v).astype(BF16)
    o_ref[...] = x + _dot(att_ref[...], wo_ref[...])


def _post(x, o_hg, o_sb, w_out, q_norm, w_q, kv, w_o, *, seq, n_mem, tm=512):
    n, d = x.shape
    hg_w = o_hg.shape[1]
    per_batch = seq // tm
    return pl.pallas_call(
        functools.partial(_post_kernel, hg_w=hg_w, heads=MEM_HEADS),
        grid=(n // tm,),
        in_specs=[
            pl.BlockSpec((tm, d), lambda i: (i, 0)),
            pl.BlockSpec((tm, hg_w), lambda i: (i, 0)),
            pl.BlockSpec((tm, o_sb.shape[1]), lambda i: (i, 0)),
            _resident(w_out.shape),
            _resident((1, d)),
            _resident(w_q.shape),
            pl.BlockSpec((n_mem, 2 * d), lambda i: (i // per_batch, 0)),
            _resident(w_o.shape),
        ],
        out_specs=pl.BlockSpec((tm, d), lambda i: (i, 0)),
        out_shape=jax.ShapeDtypeStruct((n, d), F32),
        scratch_shapes=[pltpu.VMEM((tm, d), BF16)],
        compiler_params=_params("parallel"),
        name="post",
    )(x, o_hg, o_sb, w_out, q_norm.reshape(1, d), w_q, kv, w_o)


def kernel(x, mem, ffn1_norm, ffn1_w_gu, ffn1_w_down, mix_norm, w_in, hg_lb_raw, hg_gnorm, sb_gnorm,
           w_out, mem_q_norm, mem_kv_norm, mem_w_q, mem_w_kv, mem_w_o, ffn2_norm, ffn2_w_gu,
           ffn2_w_down, final_norm):
    batch, seq, d = x.shape
    n_mem = mem.shape[1]
    depth = ffn1_norm.shape[0]
    hg_w = hg_gnorm.shape[1]
    sb_w = sb_gnorm.shape[1]
    sb_scale = (sb_w // SB_HEADS) ** -0.5

    xs = x.reshape(batch * seq, d)
    mem2d = mem.reshape(batch * n_mem, d)
    for l in range(depth):
        xs = _ffn(xs, ffn1_norm[l], ffn1_w_gu[l].astype(BF16), ffn1_w_down[l].astype(BF16))
        hg, sbp = _in_proj(xs, mix_norm[l], w_in[l].astype(BF16),
                           hg_cols=4 * hg_w, sb_w=sb_w, sb_scale=sb_scale)
        o_hg = _hgrn(hg, hg_lb_raw, hg_gnorm[l], batch=batch, seq=seq, width=hg_w, layer=l)
        o_sb = _sb(sbp, sb_gnorm[l], batch=batch, seq=seq, width=sb_w)
        kv = _mem_kv(mem2d, mem_kv_norm[l], mem_w_kv[l].astype(BF16))
        xs = _post(xs, o_hg, o_sb, w_out[l].astype(BF16), mem_q_norm[l], mem_w_q[l].astype(BF16),
                   kv, mem_w_o[l].astype(BF16), seq=seq, n_mem=n_mem)
        last = l == depth - 1
        xs = _ffn(xs, ffn2_norm[l], ffn2_w_gu[l].astype(BF16), ffn2_w_down[l].astype(BF16),
                  final_norm if last else None)
    return xs.reshape(batch, seq, d)
```

```python
import functools

import jax
import jax.numpy as jnp
from jax import lax
from jax.experimental import pallas as pl
from jax.experimental.pallas import tpu as pltpu

F32 = jnp.float32
BF16 = jnp.bfloat16

EPS = 1e-6
CHUNK = 64
SUB = 16
HG_HEADS = 4
SB_HEADS = 8
MEM_HEADS = 4
HG_SAFE_DECAY = 80.0
SB_LOG_ZERO = -105.0
LOG2E = 1.4426950408889634

V7X_VMEM_LIMIT = 56 * 1024 * 1024


def _params(*sem):
    return pltpu.CompilerParams(dimension_semantics=sem, vmem_limit_bytes=V7X_VMEM_LIMIT)


def _resident(shape):
    nd = len(shape)
    return pl.BlockSpec(shape, lambda *_: (0,) * nd, pipeline_mode=pl.Buffered(1))


def _rmsnorm(x, g):
    return x * lax.rsqrt(jnp.mean(x * x, axis=-1, keepdims=True) + EPS) * g


def _dot(a, b):
    return jnp.dot(a, b, preferred_element_type=F32)


def _dot_nt(a, b):
    return lax.dot_general(a, b, (((1,), (1,)), ((), ())), preferred_element_type=F32)


def _split3(x):
    hi = x.astype(BF16)
    r = x - hi.astype(F32)
    mid = r.astype(BF16)
    lo = (r - mid.astype(F32)).astype(BF16)
    return hi, mid, lo


def _ffn_kernel(x_ref, g_ref, wgu_ref, wd_ref, *rest, d_ff, ck, final):
    if final:
        fg_ref, o_ref, act_ref = rest
    else:
        o_ref, act_ref = rest
    x = x_ref[...]
    h = _rmsnorm(x, g_ref[...]).astype(BF16)
    for c in range(d_ff // ck):
        gate = _dot(h, wgu_ref[:, c * ck:(c + 1) * ck])
        up = _dot(h, wgu_ref[:, d_ff + c * ck:d_ff + (c + 1) * ck])
        silu = gate / (1.0 + jnp.exp(-gate))
        act_ref[:, c * ck:(c + 1) * ck] = (silu * up).astype(BF16)
    y = _dot(act_ref[...], wd_ref[...])
    out = x + 0.5 * y
    if final:
        out = _rmsnorm(out, fg_ref[...])
    o_ref[...] = out


def _ffn(x, norm_g, w_gu, w_down, final_g=None, *, tm=1024, ck=256):
    n, d = x.shape
    d_ff = w_down.shape[0]
    final = final_g is not None
    in_specs = [
        pl.BlockSpec((tm, d), lambda i: (i, 0)),
        _resident((1, d)),
        _resident((d, 2 * d_ff)),
        _resident((d_ff, d)),
    ]
    args = [x, norm_g.reshape(1, d), w_gu, w_down]
    if final:
        in_specs.append(_resident((1, d)))
        args.append(final_g.reshape(1, d))
    return pl.pallas_call(
        functools.partial(_ffn_kernel, d_ff=d_ff, ck=ck, final=final),
        grid=(n // tm,),
        in_specs=in_specs,
        out_specs=pl.BlockSpec((tm, d), lambda i: (i, 0)),
        out_shape=jax.ShapeDtypeStruct((n, d), F32),
        scratch_shapes=[pltpu.VMEM((tm, d_ff), BF16)],
        compiler_params=_params("parallel"),
        name="ffn_final" if final else "ffn",
    )(*args)


def _inproj_kernel(x_ref, g_ref, w_ref, hg_ref, sb_ref, *, hg_cols, sb_w, sb_scale, ck):
    h = _rmsnorm(x_ref[...], g_ref[...]).astype(BF16)
    for c in range(hg_cols // ck):
        hg_ref[:, c * ck:(c + 1) * ck] = _dot(h, w_ref[:, c * ck:(c + 1) * ck])
    for c in range(3 * sb_w // ck):
        p = _dot(h, w_ref[:, hg_cols + c * ck:hg_cols + (c + 1) * ck])
        if (c + 1) * ck <= sb_w:
            p = p * sb_scale
        sb_ref[:, c * ck:(c + 1) * ck] = p.astype(BF16)


def _in_proj(x, norm_g, w_in, *, hg_cols, sb_w, sb_scale, tm=512, ck=512):
    n, d = x.shape
    return pl.pallas_call(
        functools.partial(_inproj_kernel, hg_cols=hg_cols, sb_w=sb_w, sb_scale=sb_scale, ck=ck),
        grid=(n // tm,),
        in_specs=[
            pl.BlockSpec((tm, d), lambda i: (i, 0)),
            _resident((1, d)),
            _resident(w_in.shape),
        ],
        out_specs=[
            pl.BlockSpec((tm, hg_cols), lambda i: (i, 0)),
            pl.BlockSpec((tm, 3 * sb_w), lambda i: (i, 0)),
        ],
        out_shape=[
            jax.ShapeDtypeStruct((n, hg_cols), F32),
            jax.ShapeDtypeStruct((n, 3 * sb_w), BF16),
        ],
        compiler_params=_params("parallel"),
        name="in_proj",
    )(x, norm_g.reshape(1, d), w_in)


def _hgrn_kernel(q_ref, f_ref, i_ref, g_ref, lbraw_ref, gn_ref, tri_ref, o_ref,
                 st_ref, b_ref, kk_ref, safe_ref, *, rows, hd, layer, nblk):
    t = pl.program_id(1)
    cur = lax.rem(t, 2)
    prev = 1 - cur
    n_chunks = rows // CHUNK
    n_heads = q_ref.shape[1] // hd
    n_sub = CHUNK // SUB

    @pl.when(t == 0)
    def _():
        st_ref[...] = jnp.zeros_like(st_ref)
        b_ref[...] = jnp.zeros_like(b_ref)
        kk_ref[...] = jnp.zeros_like(kk_ref)
        safe_ref[1] = 1

    gn = gn_ref[...]

    def prepare():
        raw = lbraw_ref[...]
        e = jnp.exp(raw - jnp.max(raw, axis=0, keepdims=True))
        lb = jnp.sum(e[0:layer + 1, :], axis=0, keepdims=True) / jnp.sum(e, axis=0, keepdims=True)
        tri = tri_ref[...]
        f = lb + (1.0 - lb) / (1.0 + jnp.exp2(f_ref[...] * (-LOG2E)))
        kk_ref[cur] = 1.0 - f
        logf = jnp.log(f)
        hi = logf.astype(BF16)
        lo = (logf - hi.astype(F32)).astype(BF16)
        b_min = None
        for c in range(n_chunks):
            rs = slice(c * CHUNK, (c + 1) * CHUNK)
            b_c = _dot(tri, hi[rs]) + _dot(tri, lo[rs])
            b_ref[cur, rs, :] = b_c
            m = jnp.min(b_c[CHUNK - 1:CHUNK, :])
            b_min = m if b_min is None else jnp.minimum(b_min, m)
        safe_ref[cur] = (b_min >= -HG_SAFE_DECAY).astype(jnp.int32)

    def head_norm(o):
        return o * lax.rsqrt(jnp.mean(o * o, axis=-1, keepdims=True) + EPS)

    def silu(x):
        return x / (1.0 + jnp.exp2(x * (-LOG2E)))

    @pl.when(safe_ref[prev] == 1)
    def _():
        assert n_chunks == 4
        b, kk = b_ref[prev], kk_ref[prev]
        q, v, gate = q_ref[...], i_ref[...], g_ref[...]
        last = [b[(c + 1) * CHUNK - 1:(c + 1) * CHUNK, :] for c in range(n_chunks)]

        def per_chunk(vals):
            return jnp.concatenate([jnp.broadcast_to(x, (CHUNK, x.shape[1])) for x in vals], axis=0)

        one = jnp.ones_like(last[0])
        el = [jnp.exp(x) for x in last]
        eb = jnp.exp(b)
        qa = q * eb
        ka = kk / eb
        kd = ka * per_chunk(el)
        q0 = qa * per_chunk([one, el[0], el[0] * el[1], el[0] * el[1] * el[2]])
        kend = kd * per_chunk([el[1] * el[2] * el[3], el[2] * el[3], el[3], one])
        q2 = qa[3 * CHUNK:] * el[2]
        kd2 = kd[:CHUNK] * el[1]
        e_tot = el[0] * el[1] * el[2] * el[3]
        qa, ka, kd, q0, kend, q2, kd2 = (x.astype(BF16) for x in (qa, ka, kd, q0, kend, q2, kd2))
        q2 = jnp.concatenate([qa[:3 * CHUNK], q2], axis=0)
        kd2 = jnp.concatenate([kd2, kd[CHUNK:]], axis=0)
        v16 = v.astype(BF16)

        ri = lax.broadcasted_iota(jnp.int32, (rows, rows), 0)
        ci = lax.broadcasted_iota(jnp.int32, (rows, rows), 1)
        rc, cc = ri // CHUNK, ci // CHUNK
        same_chunk = (rc == cc) & (ci <= ri)
        pair_chunk = (rc == cc + 1) & (rc % 2 == 1)
        half_block = (rc >= 2) & (cc < 2)
        outs = []
        for h in range(n_heads):
            sl = slice(h * hd, (h + 1) * hd)
            a = jnp.where(same_chunk, _dot_nt(qa[:, sl], ka[:, sl]),
                          jnp.where(pair_chunk, _dot_nt(qa[:, sl], kd[:, sl]),
                                    jnp.where(half_block, _dot_nt(q2[:, sl], kd2[:, sl]), 0.0)))
            st = st_ref[h]
            o = _dot(a.astype(BF16), v16[:, sl]) + _dot_nt(q0[:, sl], st.astype(BF16))
            st_ref[h] = st * e_tot[:, sl] + _dot(v[:, sl].T.astype(BF16), kend[:, sl])
            outs.append(head_norm(o))
        y = jnp.concatenate(outs, axis=-1) * gn
        o_ref[...] = (y * silu(gate)).astype(o_ref.dtype)
        prepare()

    @pl.when(safe_ref[prev] != 1)
    def _():
        lane = lax.broadcasted_iota(jnp.int32, (SUB, CHUNK), 1)
        row = lax.broadcasted_iota(jnp.int32, (SUB, CHUNK), 0)

        def chunk_body(c, carry):
            r0 = pl.multiple_of(c * CHUNK, CHUNK)
            b = b_ref[prev, pl.ds(r0, CHUNK), :]
            kk = kk_ref[prev, pl.ds(r0, CHUNK), :]
            q = q_ref[pl.ds(r0, CHUNK), :]
            v = i_ref[pl.ds(r0, CHUNK), :]
            gate = g_ref[pl.ds(r0, CHUNK), :]
            b_last = b[CHUNK - 1:CHUNK, :]
            q_in = (q * jnp.exp(b)).astype(BF16)
            k_dec = (kk * jnp.exp(b_last - b)).astype(BF16)
            e_last = jnp.exp(b_last)
            v16 = v.astype(BF16)
            outs = []
            for h in range(n_heads):
                sl = slice(h * hd, (h + 1) * hd)
                qh, kh, bh, vh = q[:, sl], kk[:, sl], b[:, sl], v16[:, sl]
                vt = v[:, sl].T.astype(BF16)
                st = st_ref[h]
                o_inter = _dot_nt(q_in[:, sl], st.astype(BF16))
                a_rows = []
                for i in range(n_sub):
                    rs = slice(i * SUB, (i + 1) * SUB)
                    qi, ki, bi = qh[rs], kh[rs], bh[rs]
                    dwide = jnp.zeros((SUB, CHUNK), F32)
                    for s in range(SUB):
                        p = qi * (ki[s:s + 1] * jnp.exp(jnp.minimum(bi - bi[s:s + 1], 0.0)))
                        col = jnp.sum(p, axis=-1, keepdims=True)
                        dwide = jnp.where(lane == i * SUB + s, col, dwide)
                    a_i = jnp.where(lane <= i * SUB + row, dwide, 0.0)
                    if i > 0:
                        r_i = bh[i * SUB - 1:i * SUB]
                        qt = (qi * jnp.exp(bi - r_i)).astype(BF16)
                        kt = (kh * jnp.exp(jnp.minimum(r_i - bh, 0.0))).astype(BF16)
                        a_i = jnp.where(lane < i * SUB, _dot_nt(qt, kt), a_i)
                    a_rows.append(a_i)
                a = jnp.concatenate(a_rows, axis=0).astype(BF16)
                o = _dot(a, vh) + o_inter
                st_ref[h] = st * e_last[:, sl] + _dot(vt, k_dec[:, sl])
                outs.append(head_norm(o))
            y = jnp.concatenate(outs, axis=-1) * gn
            o_ref[pl.ds(r0, CHUNK), :] = (y * silu(gate)).astype(o_ref.dtype)
            return carry

        lax.fori_loop(0, n_chunks, chunk_body, 0)
        prepare()


def _hgrn(hg, lb_raw, gnorm, *, batch, seq, width, layer, rows=256):
    n = batch * seq
    nblk = seq // rows
    hd = width // HG_HEADS
    tri = jnp.tril(jnp.ones((CHUNK, CHUNK), BF16))

    def col_spec(j, lag):
        return pl.BlockSpec((rows, width),
                            lambda b, t: (b * nblk + jnp.clip(t - lag, 0, nblk - 1), j))

    return pl.pallas_call(
        functools.partial(_hgrn_kernel, rows=rows, hd=hd, layer=layer, nblk=nblk),
        grid=(batch, nblk + 1),
        in_specs=[col_spec(0, 1), col_spec(1, 0), col_spec(2, 1), col_spec(3, 1),
                  _resident(lb_raw.shape), _resident((1, width)), _resident((CHUNK, CHUNK))],
        out_specs=col_spec(0, 1),
        out_shape=jax.ShapeDtypeStruct((n, width), BF16),
        scratch_shapes=[pltpu.VMEM((HG_HEADS, hd, hd), F32),
                        pltpu.VMEM((2, rows, width), F32),
                        pltpu.VMEM((2, rows, width), F32),
                        pltpu.SMEM((2,), jnp.int32)],
        compiler_params=_params("parallel", "arbitrary"),
        name="hgrn",
    )(hg, hg, hg, hg, lb_raw, gnorm.reshape(1, width), tri)


def _sb_kernel(q_ref, k_ref, v_ref, gn_ref, uu_ref, o_ref,
               z_ref, w_ref, rem_ref, acc_ref, *, tq, hd, rs, win, nq):
    n_sub = tq // rs
    rows = 2 * rs
    uu = uu_ref[...]
    gn = gn_ref[...]
    lane = lax.broadcasted_iota(jnp.int32, (rs, 2 * hd), 1)
    lane_q = lax.broadcasted_iota(jnp.int32, (tq, 2 * hd), 1)
    col = lax.broadcasted_iota(jnp.int32, (rows, win), 1)
    col_minus_row = col - lax.rem(lax.broadcasted_iota(jnp.int32, (rows, win), 0), rs)

    z_ref[...] = jnp.zeros_like(z_ref)
    w_ref[...] = jnp.zeros_like(w_ref)
    rem_ref[...] = jnp.full_like(rem_ref, -1e30)
    acc_ref[...] = jnp.zeros_like(acc_ref)

    def frontier(t0, r, it):
        return pl.multiple_of(jnp.maximum(t0 + (r + 1) * rs - (it + 1) * win, 0), rs)

    def scores(t0, r, start):
        qr = q_ref[pl.ds(pl.multiple_of(t0 + r * rs, rs), rs), :]
        zero = jnp.zeros_like(qr)
        qq = jnp.concatenate([jnp.where(lane < hd, qr, zero), jnp.where(lane < hd, zero, qr)], axis=0)
        return _dot_nt(qq, k_ref[pl.ds(start, win), :])

    def log_weights(zs, valids, rem):
        z = jnp.concatenate([jnp.where(valids[r], zs[r], -1e30) for r in range(n_sub)], axis=0)
        sp = jnp.maximum(z, 0.0) + jnp.log(1.0 + jnp.exp2(jnp.abs(z) * (-LOG2E)))
        c = _dot(sp.astype(BF16), uu)
        if rem is not None:
            c = c + rem
        return z + c, c[:, 0:1]

    def output(lw, start):
        pv = _dot(jnp.exp2(lw * LOG2E).astype(BF16), v_ref[pl.ds(start, win), :])
        return jnp.where(lane < hd, pv[:rs], pv[rs:])

    def step(t, carry):
        slot_a = lax.rem(t, 2)
        slot_b = 1 - slot_a
        t0d = pl.multiple_of(jnp.clip(t - 3, 0, nq - 1) * tq, tq)
        o = acc_ref[slot_b].reshape(tq, 2 * hd)
        o2 = o * o
        ms0 = jnp.sum(jnp.where(lane_q < hd, o2, 0.0), axis=-1, keepdims=True)
        ms1 = jnp.sum(jnp.where(lane_q < hd, 0.0, o2), axis=-1, keepdims=True)
        ms = jnp.where(lane_q < hd, ms0, ms1) * (1.0 / hd)
        o_ref[pl.ds(t0d, tq), :] = (o * lax.rsqrt(ms + EPS) * gn).astype(o_ref.dtype)

        t0c = jnp.clip(t - 2, 0, nq - 1) * tq
        live = (jnp.max(rem_ref[slot_a]) >= SB_LOG_ZERO).astype(jnp.int32)
        for r in range(n_sub):
            acc_ref[slot_a, r] = output(w_ref[slot_a, r * rows:(r + 1) * rows, :], frontier(t0c, r, 0))

        t0b = jnp.clip(t - 1, 0, nq - 1) * tq
        zs = [z_ref[slot_b, r * rows:(r + 1) * rows, :] for r in range(n_sub)]
        valids = [col_minus_row < t0b + r * rs - frontier(t0b, r, 0) for r in range(n_sub)]
        lw, rem = log_weights(zs, valids, None)
        w_ref[slot_b] = lw
        rem_ref[slot_b] = rem

        t0a = jnp.minimum(t, nq - 1) * tq
        for r in range(n_sub):
            z_ref[slot_a, r * rows:(r + 1) * rows, :] = scores(t0a, r, frontier(t0a, r, 0))

        def cond(c):
            it, live = c
            return (frontier(t0c, n_sub - 1, it - 1) > 0) & (live > 0)

        def body(c):
            it, _ = c
            starts = [frontier(t0c, r, it) for r in range(n_sub)]
            zs = [scores(t0c, r, starts[r]) for r in range(n_sub)]
            valids = [col < frontier(t0c, r, it - 1) - starts[r] for r in range(n_sub)]
            lw, rem = log_weights(zs, valids, rem_ref[slot_a])
            for r in range(n_sub):
                acc_ref[slot_a, r] += output(lw[r * rows:(r + 1) * rows], starts[r])
            rem_ref[slot_a] = rem
            return it + 1, (jnp.max(rem) >= SB_LOG_ZERO).astype(jnp.int32)

        lax.while_loop(cond, body, (jnp.int32(1), live))
        return carry

    lax.fori_loop(0, nq + 3, step, 0)


def _sb(sbp, gnorm, *, batch, seq, width, tq=256, rs=64, win=256):
    n = batch * seq
    hd = width // SB_HEADS
    pair = 2 * hd
    n_pair = width // pair
    nq = seq // tq
    n_sub = tq // rs
    u = -(jnp.arange(win)[:, None] >= jnp.arange(win)[None, :]).astype(BF16)

    def seq_spec(group):
        return pl.BlockSpec((seq, pair), lambda b, p: (b, group * n_pair + p))

    return pl.pallas_call(
        functools.partial(_sb_kernel, tq=tq, hd=hd, rs=rs, win=win, nq=nq),
        grid=(batch, n_pair),
        in_specs=[
            seq_spec(0), seq_spec(1), seq_spec(2),
            pl.BlockSpec((1, pair), lambda b, p: (0, p)),
            _resident((win, win)),
        ],
        out_specs=seq_spec(0),
        out_shape=jax.ShapeDtypeStruct((n, width), BF16),
        scratch_shapes=[
            pltpu.VMEM((2, n_sub * 2 * rs, win), F32),
            pltpu.VMEM((2, n_sub * 2 * rs, win), F32),
            pltpu.VMEM((2, n_sub * 2 * rs, 1), F32),
            pltpu.VMEM((2, n_sub, rs, pair), F32),
        ],
        compiler_params=_params("parallel", "parallel"),
        name="sb",
    )(sbp, sbp, sbp, gnorm.reshape(1, width), u)


def _memkv_kernel(m_ref, g_ref, w_ref, o_ref):
    h = _rmsnorm(m_ref[...], g_ref[...]).astype(BF16)
    o_ref[...] = _dot(h, w_ref[...]).astype(o_ref.dtype)


def _mem_kv(mem2d, norm_g, w_kv, *, tm=256):
    n, d = mem2d.shape
    return pl.pallas_call(
        _memkv_kernel,
        grid=(n // tm,),
        in_specs=[pl.BlockSpec((tm, d), lambda i: (i, 0)), _resident((1, d)), _resident(w_kv.shape)],
        out_specs=pl.BlockSpec((tm, w_kv.shape[1]), lambda i: (i, 0)),
        out_shape=jax.ShapeDtypeStruct((n, w_kv.shape[1]), BF16),
        compiler_params=_params("parallel"),
        name="mem_kv",
    )(mem2d, norm_g.reshape(1, d), w_kv)


def _post_kernel(x_ref, ohg_ref, osb_ref, wout_ref, qg_ref, wq_ref, kv_ref, wo_ref, o_ref, att_ref,
                 *, hg_w, heads):
    d = x_ref.shape[1]
    dh = d // heads
    x = x_ref[...] + _dot(ohg_ref[...], wout_ref[:hg_w, :]) + _dot(osb_ref[...], wout_ref[hg_w:, :])
    hq = _rmsnorm(x, qg_ref[...]).astype(BF16)
    q = (_dot(hq, wq_ref[...]) * (dh ** -0.5)).astype(BF16)
    for h in range(heads):
        k = kv_ref[:, h * dh:(h + 1) * dh]
        v = kv_ref[:, d + h * dh:d + (h + 1) * dh]
        s = _dot_nt(q[:, h * dh:(h + 1) * dh], k)
        e = jnp.exp(s - jnp.max(s, axis=-1, keepdims=True))
        p = e / jnp.sum(e, axis=-1, keepdims=True)
        att_ref[:, h * dh:(h + 1) * dh] = _dot(p.astype(BF16), v).astype(BF16)
    o_ref[...] = x + _dot(att_ref[...], wo_ref[...])


def _post(x, o_hg, o_sb, w_out, q_norm, w_q, kv, w_o, *, seq, n_mem, tm=1024):
    n, d = x.shape
    hg_w = o_hg.shape[1]
    per_batch = seq // tm
    return pl.pallas_call(
        functools.partial(_post_kernel, hg_w=hg_w, heads=MEM_HEADS),
        grid=(n // tm,),
        in_specs=[
            pl.BlockSpec((tm, d), lambda i: (i, 0)),
            pl.BlockSpec((tm, hg_w), lambda i: (i, 0)),
            pl.BlockSpec((tm, o_sb.shape[1]), lambda i: (i, 0)),
            _resident(w_out.shape),
            _resident((1, d)),
            _resident(w_q.shape),
            pl.BlockSpec((n_mem, 2 * d), lambda i: (i // per_batch, 0)),
            _resident(w_o.shape),
        ],
        out_specs=pl.BlockSpec((tm, d), lambda i: (i, 0)),
        out_shape=jax.ShapeDtypeStruct((n, d), F32),
        scratch_shapes=[pltpu.VMEM((tm, d), BF16)],
        compiler_params=_params("parallel"),
        name="post",
    )(x, o_hg, o_sb, w_out, q_norm.reshape(1, d), w_q, kv, w_o)


def kernel(x, mem, ffn1_norm, ffn1_w_gu, ffn1_w_down, mix_norm, w_in, hg_lb_raw, hg_gnorm, sb_gnorm,
           w_out, mem_q_norm, mem_kv_norm, mem_w_q, mem_w_kv, mem_w_o, ffn2_norm, ffn2_w_gu,
           ffn2_w_down, final_norm):
    batch, seq, d = x.shape
    n_mem = mem.shape[1]
    depth = ffn1_norm.shape[0]
    hg_w = hg_gnorm.shape[1]
    sb_w = sb_gnorm.shape[1]
    sb_scale = (sb_w // SB_HEADS) ** -0.5

    xs = x.reshape(batch * seq, d)
    mem2d = mem.reshape(batch * n_mem, d)
    for l in range(depth):
        xs = _ffn(xs, ffn1_norm[l], ffn1_w_gu[l].astype(BF16), ffn1_w_down[l].astype(BF16))
        hg, sbp = _in_proj(xs, mix_norm[l], w_in[l].astype(BF16),
                           hg_cols=4 * hg_w, sb_w=sb_w, sb_scale=sb_scale)
        o_hg = _hgrn(hg, hg_lb_raw, hg_gnorm[l], batch=batch, seq=seq, width=hg_w, layer=l)
        o_sb = _sb(sbp, sb_gnorm[l], batch=batch, seq=seq, width=sb_w)
        kv = _mem_kv(mem2d, mem_kv_norm[l], mem_w_kv[l].astype(BF16))
        xs = _post(xs, o_hg, o_sb, w_out[l].astype(BF16), mem_q_norm[l], mem_w_q[l].astype(BF16),
                   kv, mem_w_o[l].astype(BF16), seq=seq, n_mem=n_mem)
        last = l == depth - 1
        xs = _ffn(xs, ffn2_norm[l], ffn2_w_gu[l].astype(BF16), ffn2_w_down[l].astype(BF16),
                  final_norm if last else None)
    return xs.reshape(batch, seq, d)
```
